```python
import math
import jax, jax.numpy as jnp
from jax import lax
import numpy as np

D_MODEL = 2048
BATCH = 4
SEQ = 4096
DEPTH = 1

D_FF = 5632
ATT_HEADS = 8
ATT_HEAD_DIM = 128
ATT_V_DIM = 2 * ATT_HEAD_DIM
ATT_QK_WIDTH = ATT_HEADS * 2 * ATT_HEAD_DIM
ATT_V_WIDTH = ATT_HEADS * ATT_V_DIM
Q_BLOCK = 128
SSM_EXPAND = 2
SSM_D_INNER = SSM_EXPAND * D_MODEL
SSM_HEAD_DIM = 64
SSM_HEADS = SSM_D_INNER // SSM_HEAD_DIM
SSM_GROUPS = 8
SSM_D_STATE = 128
SSM_CONV = 4
SSM_CHUNK = 128
SSM_BC_WIDTH = SSM_GROUPS * SSM_D_STATE
SSM_CONV_DIM = SSM_D_INNER + 2 * SSM_BC_WIDTH
DT_MIN = 0.001
DT_MAX = 0.1
N_BRANCH = 2
IN_PROJ_WIDTH = 2 * ATT_QK_WIDTH + ATT_V_WIDTH + SSM_D_INNER + SSM_CONV_DIM + SSM_HEADS + N_BRANCH * D_MODEL
RMS_EPS = 1e-6

kernel_name = "hybrid_diffattn_mamba2_macaron"


def rms_norm(x, g, eps=RMS_EPS):
    xf = x.astype(jnp.float32)
    y = xf * lax.rsqrt(jnp.mean(xf * xf, axis=-1, keepdims=True) + eps)
    return (y * g.astype(jnp.float32)).astype(x.dtype)


def swiglu(x, w_gate, w_up, w_down):
    return (jax.nn.silu(x @ w_gate) * (x @ w_up)) @ w_down


def diff_attention(q, k, v, lam, subln_g, lambda_init):
    b, s = q.shape[0], q.shape[1]
    q = q * (ATT_HEAD_DIM ** -0.5)
    outs = []
    for i in range(s // Q_BLOCK):
        q0 = i * Q_BLOCK
        kv_end = q0 + Q_BLOCK
        qb = q[:, q0:kv_end]
        kb = k[:, :kv_end]
        vb = v[:, :kv_end]
        scores = jnp.einsum('bqhmd,bkhmd->bhmqk', qb, kb).astype(jnp.float32)
        q_pos = q0 + jnp.arange(Q_BLOCK)
        k_pos = jnp.arange(kv_end)
        causal = k_pos[None, :] <= q_pos[:, None]
        scores = jnp.where(causal, scores, -jnp.inf)
        probs = jax.nn.softmax(scores, axis=-1)
        diff = probs[:, :, 0] - lam * probs[:, :, 1]
        outs.append(jnp.einsum('bhqk,bkhv->bqhv', diff.astype(vb.dtype), vb))
    o = jnp.concatenate(outs, axis=1)
    o = rms_norm(o, subln_g) * (1.0 - lambda_init)
    return o.reshape(b, s, ATT_V_WIDTH)


def causal_depthwise_conv(x, w, bias):
    c = x.shape[-1]
    y = lax.conv_general_dilated(x, w[:, None, :].astype(x.dtype), window_strides=(1,),
                                 padding=[(SSM_CONV - 1, 0)],
                                 dimension_numbers=('NWC', 'WIO', 'NWC'),
                                 feature_group_count=c)
    return y + bias


def segsum_exp(a_cs):
    n = a_cs.shape[-1]
    mask = jnp.tril(jnp.ones((n, n), dtype=bool))
    diff = a_cs[..., :, None] - a_cs[..., None, :]
    return jnp.exp(jnp.where(mask, diff, -jnp.inf))


def ssd_chunked(x, dt, a, bmat, cmat, d_skip):
    b, s, h, p = x.shape
    g, n = bmat.shape[-2], bmat.shape[-1]
    e = h // g
    c, l = s // SSM_CHUNK, SSM_CHUNK
    xf = x.astype(jnp.float32)
    xdt = (xf * dt[..., None]).reshape(b, c, l, g, e, p)
    a_cs = jnp.cumsum((dt * a).reshape(b, c, l, g, e).transpose(0, 1, 3, 4, 2), axis=-1)
    bc = bmat.astype(jnp.float32).reshape(b, c, l, g, n)
    cc = cmat.astype(jnp.float32).reshape(b, c, l, g, n)
    cb = jnp.einsum('bclgn,bcsgn->bcgls', cc, bc)
    m = cb[:, :, :, None] * segsum_exp(a_cs)
    y_diag = jnp.einsum('bcgels,bcsgep->bclgep', m, xdt)
    decay_to_end = jnp.exp(a_cs[..., -1:] - a_cs).transpose(0, 1, 4, 2, 3)
    chunk_states = jnp.einsum('bclgn,bclgep->bcgepn', bc, xdt * decay_to_end[..., None])
    chunk_decay = jnp.exp(a_cs[..., -1])

    def step(state, inp):
        st_c, dec_c = inp
        return state * dec_c[..., None, None] + st_c, state

    init = jnp.zeros((b, g, e, p, n), jnp.float32)
    _, prev = lax.scan(step, init, (chunk_states.transpose(1, 0, 2, 3, 4, 5),
                                    chunk_decay.transpose(1, 0, 2, 3)))
    prev = prev.transpose(1, 0, 2, 3, 4, 5)
    decay_in = jnp.exp(a_cs).transpose(0, 1, 4, 2, 3)
    y_off = jnp.einsum('bclgn,bcgepn->bclgep', cc, prev) * decay_in[..., None]
    y = (y_diag + y_off).reshape(b, s, h, p) + xf * d_skip.astype(jnp.float32)[:, None]
    return y


def mamba2_mixer(z, xbc, dt_raw, conv_w, conv_b, dt_bias, a_log, d_skip, norm_g):
    b, s = z.shape[0], z.shape[1]
    xbc = jax.nn.silu(causal_depthwise_conv(xbc, conv_w, conv_b))
    xs, bs, cs = jnp.split(xbc, [SSM_D_INNER, SSM_D_INNER + SSM_BC_WIDTH], axis=-1)
    xs = xs.reshape(b, s, SSM_HEADS, SSM_HEAD_DIM)
    bs = bs.reshape(b, s, SSM_GROUPS, SSM_D_STATE)
    cs = cs.reshape(b, s, SSM_GROUPS, SSM_D_STATE)
    dt = jax.nn.softplus(dt_raw.astype(jnp.float32) + dt_bias.astype(jnp.float32))
    a = -jnp.exp(a_log.astype(jnp.float32))
    y = ssd_chunked(xs, dt, a, bs, cs, d_skip)
    y = y.reshape(b, s, SSM_D_INNER) * jax.nn.silu(z.astype(jnp.float32))
    y = rms_norm(y.reshape(b, s, SSM_GROUPS, SSM_D_INNER // SSM_GROUPS),
                 norm_g.reshape(SSM_GROUPS, SSM_D_INNER // SSM_GROUPS))
    return y.reshape(b, s, SSM_D_INNER).astype(z.dtype)


def setup_inputs(seed: int = 0) -> dict:
    key = jax.random.key(seed)
    ks = jax.random.split(key, 32)
    f32 = jnp.float32

    def normal(k, shape, scale):
        return jax.random.normal(k, shape, f32) * scale

    def gain(k, dim):
        return 1.0 + 0.02 * jax.random.normal(k, (DEPTH, dim), f32)

    dt0 = jnp.exp(jax.random.uniform(ks[20], (DEPTH, SSM_HEADS), f32, math.log(DT_MIN), math.log(DT_MAX)))
    return {
        "x": jax.random.normal(ks[0], (BATCH, SEQ, D_MODEL), f32),
        "ffn1_pre_g": gain(ks[1], D_MODEL),
        "ffn1_w_gate": normal(ks[2], (DEPTH, D_MODEL, D_FF), D_MODEL ** -0.5),
        "ffn1_w_up": normal(ks[3], (DEPTH, D_MODEL, D_FF), D_MODEL ** -0.5),
        "ffn1_w_down": normal(ks[4], (DEPTH, D_FF, D_MODEL), D_FF ** -0.5),
        "ffn1_post_g": gain(ks[5], D_MODEL),
        "mix_pre_g": gain(ks[6], D_MODEL),
        "w_in": normal(ks[7], (DEPTH, D_MODEL, IN_PROJ_WIDTH), D_MODEL ** -0.5),
        "att_lambda_q1": normal(ks[8], (DEPTH, ATT_HEAD_DIM), 0.1),
        "att_lambda_k1": normal(ks[9], (DEPTH, ATT_HEAD_DIM), 0.1),
        "att_lambda_q2": normal(ks[10], (DEPTH, ATT_HEAD_DIM), 0.1),
        "att_lambda_k2": normal(ks[11], (DEPTH, ATT_HEAD_DIM), 0.1),
        "att_subln_g": gain(ks[12], ATT_V_DIM),
        "ssm_conv_w": normal(ks[13], (DEPTH, SSM_CONV, SSM_CONV_DIM), SSM_CONV ** -0.5),
        "ssm_conv_b": normal(ks[14], (DEPTH, SSM_CONV_DIM), 0.01),
        "ssm_dt_bias": dt0 + jnp.log(-jnp.expm1(-dt0)),
        "ssm_a_log": jnp.log(jax.random.uniform(ks[15], (DEPTH, SSM_HEADS), f32, 1.0, 16.0)),
        "ssm_d": 1.0 + 0.01 * jax.random.normal(ks[16], (DEPTH, SSM_HEADS), f32),
        "ssm_norm_g": gain(ks[17], SSM_D_INNER),
        "w_branch_att": normal(ks[18], (DEPTH, ATT_V_WIDTH, D_MODEL), ATT_V_WIDTH ** -0.5),
        "w_branch_ssm": normal(ks[19], (DEPTH, SSM_D_INNER, D_MODEL), SSM_D_INNER ** -0.5),
        "w_out": normal(ks[21], (DEPTH, D_MODEL, D_MODEL), D_MODEL ** -0.5),
        "mix_post_g": gain(ks[22], D_MODEL),
        "ffn2_pre_g": gain(ks[23], D_MODEL),
        "ffn2_w_gate": normal(ks[24], (DEPTH, D_MODEL, D_FF), D_MODEL ** -0.5),
        "ffn2_w_up": normal(ks[25], (DEPTH, D_MODEL, D_FF), D_MODEL ** -0.5),
        "ffn2_w_down": normal(ks[26], (DEPTH, D_FF, D_MODEL), D_FF ** -0.5),
        "ffn2_post_g": gain(ks[27], D_MODEL),
    }


def reference(x, ffn1_pre_g, ffn1_w_gate, ffn1_w_up, ffn1_w_down, ffn1_post_g,
              mix_pre_g, w_in, att_lambda_q1, att_lambda_k1, att_lambda_q2, att_lambda_k2,
              att_subln_g, ssm_conv_w, ssm_conv_b, ssm_dt_bias, ssm_a_log, ssm_d, ssm_norm_g,
              w_branch_att, w_branch_ssm, w_out, mix_post_g,
              ffn2_pre_g, ffn2_w_gate, ffn2_w_up, ffn2_w_down, ffn2_post_g):
    b, s, _ = x.shape
    split_points = [ATT_QK_WIDTH, 2 * ATT_QK_WIDTH, 2 * ATT_QK_WIDTH + ATT_V_WIDTH,
                    2 * ATT_QK_WIDTH + ATT_V_WIDTH + SSM_D_INNER,
                    2 * ATT_QK_WIDTH + ATT_V_WIDTH + SSM_D_INNER + SSM_CONV_DIM,
                    2 * ATT_QK_WIDTH + ATT_V_WIDTH + SSM_D_INNER + SSM_CONV_DIM + SSM_HEADS]
    h = x
    for i in range(DEPTH):
        lambda_init = 0.8 - 0.6 * math.exp(-0.3 * i)
        f = swiglu(rms_norm(h, ffn1_pre_g[i]), ffn1_w_gate[i], ffn1_w_up[i], ffn1_w_down[i])
        h = h + 0.5 * rms_norm(f, ffn1_post_g[i])
        u = rms_norm(h, mix_pre_g[i])
        proj = u @ w_in[i]
        q, k, v, z, xbc, dt_raw, gates = jnp.split(proj, split_points, axis=-1)
        q = q.reshape(b, s, ATT_HEADS, 2, ATT_HEAD_DIM)
        k = k.reshape(b, s, ATT_HEADS, 2, ATT_HEAD_DIM)
        v = v.reshape(b, s, ATT_HEADS, ATT_V_DIM)
        lam = (jnp.exp(jnp.sum(att_lambda_q1[i].astype(jnp.float32) * att_lambda_k1[i].astype(jnp.float32)))
               - jnp.exp(jnp.sum(att_lambda_q2[i].astype(jnp.float32) * att_lambda_k2[i].astype(jnp.float32)))
               + lambda_init)
        o_att = diff_attention(q, k, v, lam, att_subln_g[i], lambda_init)
        o_ssm = mamba2_mixer(z, xbc, dt_raw, ssm_conv_w[i], ssm_conv_b[i], ssm_dt_bias[i],
                             ssm_a_log[i], ssm_d[i], ssm_norm_g[i])
        g = jax.nn.sigmoid(gates).reshape(b, s, N_BRANCH, D_MODEL)
        merged = g[:, :, 0] * (o_att @ w_branch_att[i]) + g[:, :, 1] * (o_ssm @ w_branch_ssm[i])
        h = h + rms_norm(merged @ w_out[i], mix_post_g[i])
        f = swiglu(rms_norm(h, ffn2_pre_g[i]), ffn2_w_gate[i], ffn2_w_up[i], ffn2_w_down[i])
        h = h + 0.5 * rms_norm(f, ffn2_post_g[i])
    return h
```

```python
import functools
import math

import jax
import jax.numpy as jnp
from jax import lax
from jax.experimental import pallas as pl
from jax.experimental.pallas import tpu as pltpu

F32 = jnp.float32
BF16 = jnp.bfloat16

V7X_SUBLANES = 8
V7X_LANES = 128
V7X_VMEM_LIMIT_BYTES = 56 * 1024 * 1024

RMS_EPS = 1e-6

ATT_HEADS = 8
ATT_HEAD_DIM = 128
ATT_V_DIM = 2 * ATT_HEAD_DIM
SSM_HEAD_DIM = 64
SSM_GROUPS = 8
SSM_D_STATE = 128
SSM_CONV = 4
SSM_CHUNK = 128
LAMBDA_INIT = 0.8 - 0.6 * math.exp(-0.3 * 0)


def _params(*semantics):
    return pltpu.CompilerParams(dimension_semantics=semantics,
                                vmem_limit_bytes=V7X_VMEM_LIMIT_BYTES)


def _rms_scale(x):
    return x * lax.rsqrt(jnp.mean(x * x, axis=-1, keepdims=True) + RMS_EPS)


def _silu(x):
    return x / (1.0 + jnp.exp(-x))


def _sigmoid(x):
    return 1.0 / (1.0 + jnp.exp(-x))


def _ffn_kernel(h_ref, pre_g_ref, wg_ref, wu_ref, wd_ref, post_g_ref, out_ref, xn_ref, acc_ref):
    j = pl.program_id(1)

    @pl.when(j == 0)
    def _():
        xn_ref[...] = (_rms_scale(h_ref[...]) * pre_g_ref[...]).astype(BF16)

    xn = xn_ref[...]
    gate = jnp.dot(xn, wg_ref[...], preferred_element_type=F32)
    up = jnp.dot(xn, wu_ref[...], preferred_element_type=F32)
    act = (_silu(gate) * up).astype(BF16)
    part = jnp.dot(act, wd_ref[...], preferred_element_type=F32)

    @pl.when(j == 0)
    def _():
        acc_ref[...] = part

    @pl.when(j > 0)
    def _():
        acc_ref[...] += part

    @pl.when(j == pl.num_programs(1) - 1)
    def _():
        out_ref[...] = h_ref[...] + 0.5 * (_rms_scale(acc_ref[...]) * post_g_ref[...])


def _ffn(h, pre_g, w_gate, w_up, w_down, post_g, *, tm, tf):
    t, d = h.shape
    f = w_gate.shape[1]
    assert t % tm == 0 and f % tf == 0, (t, tm, f, tf)
    return pl.pallas_call(
        _ffn_kernel,
        grid=(t // tm, f // tf),
        in_specs=[
            pl.BlockSpec((tm, d), lambda i, j: (i, 0)),
            pl.BlockSpec((1, d), lambda i, j: (0, 0)),
            pl.BlockSpec((d, tf), lambda i, j: (0, j)),
            pl.BlockSpec((d, tf), lambda i, j: (0, j)),
            pl.BlockSpec((tf, d), lambda i, j: (j, 0)),
            pl.BlockSpec((1, d), lambda i, j: (0, 0)),
        ],
        out_specs=pl.BlockSpec((tm, d), lambda i, j: (i, 0)),
        out_shape=jax.ShapeDtypeStruct((t, d), F32),
        scratch_shapes=[pltpu.VMEM((tm, d), BF16), pltpu.VMEM((tm, d), F32)],
        compiler_params=_params("parallel", "arbitrary"),
        name="ffn",
    )(h, pre_g, w_gate, w_up, w_down, post_g)


def _inproj_kernel(h_ref, g_ref, w_ref, wdt_ref, out_ref, dt_ref, xn_ref):
    j = pl.program_id(1)

    @pl.when(j == 0)
    def _():
        xn = (_rms_scale(h_ref[...]) * g_ref[...]).astype(BF16)
        xn_ref[...] = xn
        dt_ref[...] = jnp.dot(xn, wdt_ref[...], preferred_element_type=F32)

    out_ref[...] = jnp.dot(xn_ref[...], w_ref[...], preferred_element_type=F32).astype(BF16)


def _inproj(h, g, w_main, w_dt, *, tm, tn):
    t, d = h.shape
    n = w_main.shape[1]
    ndt = w_dt.shape[1]
    assert t % tm == 0 and n % tn == 0, (t, tm, n, tn)
    return pl.pallas_call(
        _inproj_kernel,
        grid=(t // tm, n // tn),
        in_specs=[
            pl.BlockSpec((tm, d), lambda i, j: (i, 0)),
            pl.BlockSpec((1, d), lambda i, j: (0, 0)),
            pl.BlockSpec((d, tn), lambda i, j: (0, j)),
            pl.BlockSpec((d, ndt), lambda i, j: (0, 0)),
        ],
        out_specs=[
            pl.BlockSpec((tm, tn), lambda i, j: (i, j)),
            pl.BlockSpec((tm, ndt), lambda i, j: (i, 0)),
        ],
        out_shape=[jax.ShapeDtypeStruct((t, n), BF16), jax.ShapeDtypeStruct((t, ndt), F32)],
        scratch_shapes=[pltpu.VMEM((tm, d), BF16)],
        compiler_params=_params("parallel", "arbitrary"),
        name="in_proj",
    )(h, g, w_main, w_dt)


def _attn_kernel(lq1_ref, lk1_ref, lq2_ref, lk2_ref, g_ref, q_ref, k_ref, v_ref, o_ref,
                 m_ref, l_ref, acc_ref, *, tq):
    qi = pl.program_id(2)
    dh = ATT_HEAD_DIM
    m_ref[...] = jnp.full(m_ref.shape, -jnp.inf, F32)
    l_ref[...] = jnp.zeros(l_ref.shape, F32)
    acc_ref[...] = jnp.zeros(acc_ref.shape, F32)

    def kv_tile(j, masked):
        start = pl.multiple_of(j * tq, tq)
        v = v_ref[0, pl.ds(start, tq), :]
        for mi in range(2):
            q = q_ref[0, :, mi * dh:(mi + 1) * dh]
            k = k_ref[0, pl.ds(start, tq), mi * dh:(mi + 1) * dh]
            s = lax.dot_general(q, k, (((1,), (1,)), ((), ())), preferred_element_type=F32)
            if masked:
                row = lax.broadcasted_iota(jnp.int32, s.shape, 0)
                col = lax.broadcasted_iota(jnp.int32, s.shape, 1)
                s = jnp.where(col <= row, s, -jnp.inf)
            m_prev = m_ref[mi]
            m_new = jnp.maximum(m_prev, jnp.max(s, axis=-1, keepdims=True))
            alpha = jnp.exp(m_prev - m_new)
            p = jnp.exp(s - m_new)
            l_ref[mi] = alpha * l_ref[mi] + jnp.sum(p, axis=-1, keepdims=True)
            acc_ref[mi] = alpha * acc_ref[mi] + jnp.dot(p.astype(BF16), v, preferred_element_type=F32)
            m_ref[mi] = m_new

    def full_tile(j, carry):
        kv_tile(j, masked=False)
        return carry

    lax.fori_loop(0, qi, full_tile, 0)
    kv_tile(qi, masked=True)

    lam = (jnp.exp(jnp.sum(lq1_ref[...] * lk1_ref[...], axis=-1, keepdims=True))
           - jnp.exp(jnp.sum(lq2_ref[...] * lk2_ref[...], axis=-1, keepdims=True))
           + LAMBDA_INIT)
    o = acc_ref[0] / l_ref[0] - lam * (acc_ref[1] / l_ref[1])
    o_ref[0] = (_rms_scale(o) * g_ref[...] * (1.0 - LAMBDA_INIT)).astype(o_ref.dtype)


def _attention(proj, lq1, lk1, lq2, lk2, subln_g, *, tq, q_col, k_col, v_col):
    b, s, _ = proj.shape
    w = ATT_V_DIM
    assert s % tq == 0, (s, tq)
    vec = pl.BlockSpec((1, ATT_HEAD_DIM), lambda bi, h, qi: (0, 0))
    return pl.pallas_call(
        functools.partial(_attn_kernel, tq=tq),
        grid=(b, ATT_HEADS, s // tq),
        in_specs=[
            vec, vec, vec, vec,
            pl.BlockSpec((1, w), lambda bi, h, qi: (0, 0)),
            pl.BlockSpec((1, tq, w), lambda bi, h, qi: (bi, qi, q_col + h)),
            pl.BlockSpec((1, s, w), lambda bi, h, qi: (bi, 0, k_col + h)),
            pl.BlockSpec((1, s, w), lambda bi, h, qi: (bi, 0, v_col + h)),
        ],
        out_specs=pl.BlockSpec((1, tq, w), lambda bi, h, qi: (bi, qi, h)),
        out_shape=jax.ShapeDtypeStruct((b, s, ATT_HEADS * w), BF16),
        scratch_shapes=[
            pltpu.VMEM((2, tq, 1), F32),
            pltpu.VMEM((2, tq, 1), F32),
            pltpu.VMEM((2, tq, w), F32),
        ],
        compiler_params=_params("parallel", "parallel", "arbitrary"),
        name="diff_attn",
    )(lq1, lk1, lq2, lk2, subln_g, proj, proj, proj)


def _split3(x):
    p1 = x.astype(BF16)
    r1 = x - p1.astype(F32)
    p2 = r1.astype(BF16)
    p3 = (r1 - p2.astype(F32)).astype(BF16)
    return p1, p2, p3


def _ssd_kernel(dt_ref, dtb_ref, alog_ref, x_ref, b_ref, c_ref, z_ref,
                wx_ref, wb_ref, wc_ref, bx_ref, bb_ref, bc_ref, dskip_ref, ng_ref,
                o_ref,
                xext_ref, bext_ref, cext_ref, state_ref, acst_ref, ydiag_ref):
    g = pl.program_id(1)
    c = pl.program_id(2)
    ln = SSM_CHUNK
    hp = SSM_HEAD_DIM
    heads = x_ref.shape[-1] // hp
    tail = V7X_SUBLANES

    @pl.when(c == 0)
    def _():
        xext_ref[0:tail, :] = jnp.zeros((tail, xext_ref.shape[1]), F32)
        bext_ref[0:tail, :] = jnp.zeros((tail, bext_ref.shape[1]), F32)
        cext_ref[0:tail, :] = jnp.zeros((tail, cext_ref.shape[1]), F32)
        state_ref[...] = jnp.zeros(state_ref.shape, F32)

    def conv_silu(ext_ref, blk_ref, w_ref, bias_ref):
        ext_ref[tail:tail + ln, :] = blk_ref[0].astype(F32)
        y = bias_ref[...]
        for k in range(SSM_CONV):
            y = y + w_ref[k:k + 1, :] * ext_ref[pl.ds(tail - (SSM_CONV - 1) + k, ln), :]
        ext_ref[0:tail, :] = ext_ref[ln:ln + tail, :]
        return _silu(y)

    xc = conv_silu(xext_ref, x_ref, wx_ref, bx_ref)
    bc = conv_silu(bext_ref, b_ref, wb_ref, bb_ref)
    cc = conv_silu(cext_ref, c_ref, wc_ref, bc_ref)

    xb = dt_ref[0] + dtb_ref[...]
    dt_all = jnp.maximum(xb, 0.0) + jnp.log1p(jnp.exp(-jnp.abs(xb)))
    adt = dt_all * (-jnp.exp(alog_ref[...]))
    row = lax.broadcasted_iota(jnp.int32, (ln, ln), 0)
    col = lax.broadcasted_iota(jnp.int32, (ln, ln), 1)
    tril = col <= row
    ltri = jnp.where(tril, 1.0, 0.0).astype(BF16)
    acs_all = sum(jnp.dot(ltri, part, preferred_element_type=F32) for part in _split3(adt))
    acst_ref[...] = acs_all.T

    shift = lax.rem(V7X_LANES - heads * g, V7X_LANES)
    dt_g = pltpu.roll(dt_all, shift, 1)
    acs_g = pltpu.roll(acs_all, shift, 1)

    lane = lax.broadcasted_iota(jnp.int32, (ln, 2 * hp), 1)
    lo = lane < hp
    cb = lax.dot_general(cc.astype(BF16), bc.astype(BF16), (((1,), (1,)), ((), ())),
                         preferred_element_type=F32)
    state = state_ref[...]
    y_off = jnp.dot(cc.astype(BF16), state.astype(BF16), preferred_element_type=F32)

    xw_parts = []
    dec_last_parts = []
    for pr in range(heads // 2):
        e0, e1 = 2 * pr, 2 * pr + 1
        sl = slice(pr * 2 * hp, (pr + 1) * 2 * hp)
        dt_pair = jnp.where(lo, dt_g[:, e0:e0 + 1], dt_g[:, e1:e1 + 1])
        acs_pair = jnp.where(lo, acs_g[:, e0:e0 + 1], acs_g[:, e1:e1 + 1])
        xdt = xc[:, sl] * dt_pair
        y_pair = y_off[:, sl] * jnp.exp(acs_pair)
        for e, keep in ((e0, lo), (e1, jnp.logical_not(lo))):
            rowv = acst_ref[pl.ds(heads * g + e, 1), :]
            colv = acs_g[:, e:e + 1]
            decay = jnp.exp(jnp.where(tril, colv - rowv, -jnp.inf))
            m = (cb * decay).astype(BF16)
            y_pair = y_pair + jnp.dot(m, jnp.where(keep, xdt, 0.0).astype(BF16),
                                      preferred_element_type=F32)
        ydiag_ref[:, sl] = y_pair
        last = acs_pair[ln - 1:ln, :]
        xw_parts.append((xdt * jnp.exp(last - acs_pair)).astype(BF16))
        dec_last_parts.append(jnp.exp(last))

    xw = jnp.concatenate(xw_parts, axis=1)
    dec_last = jnp.concatenate(dec_last_parts, axis=1)
    state_ref[...] = state * dec_last + jnp.dot(bc.T.astype(BF16), xw, preferred_element_type=F32)

    y = ydiag_ref[...] + xc * dskip_ref[...]
    zz = z_ref[0].astype(F32)
    y = y * _silu(zz)
    o_ref[0] = (_rms_scale(y) * ng_ref[...]).astype(o_ref.dtype)


def _ssd(proj, dt_raw, dt_bias, a_log, conv_w, conv_b, d_skip_row, norm_g, *,
         x_col, b_col, c_col, z_col):
    bsz, s, _ = proj.shape
    ln = SSM_CHUNK
    gw = d_skip_row.shape[1] // SSM_GROUPS
    ns = SSM_D_STATE
    assert s % ln == 0 and gw % (2 * SSM_HEAD_DIM) == 0, (s, ln, gw)
    xw_col = 0
    bw_col = (SSM_GROUPS * gw) // ns
    cw_col = bw_col + SSM_GROUPS
    row128 = pl.BlockSpec((1, V7X_LANES), lambda b, g, c: (0, 0))
    return pl.pallas_call(
        _ssd_kernel,
        grid=(bsz, SSM_GROUPS, s // ln),
        in_specs=[
            pl.BlockSpec((1, ln, V7X_LANES), lambda b, g, c: (b, c, 0)),
            row128, row128,
            pl.BlockSpec((1, ln, gw), lambda b, g, c: (b, c, x_col + g)),
            pl.BlockSpec((1, ln, ns), lambda b, g, c: (b, c, b_col + g)),
            pl.BlockSpec((1, ln, ns), lambda b, g, c: (b, c, c_col + g)),
            pl.BlockSpec((1, ln, gw), lambda b, g, c: (b, c, z_col + g)),
            pl.BlockSpec((SSM_CONV, gw), lambda b, g, c: (0, xw_col + g)),
            pl.BlockSpec((SSM_CONV, ns), lambda b, g, c: (0, bw_col + g)),
            pl.BlockSpec((SSM_CONV, ns), lambda b, g, c: (0, cw_col + g)),
            pl.BlockSpec((1, gw), lambda b, g, c: (0, xw_col + g)),
            pl.BlockSpec((1, ns), lambda b, g, c: (0, bw_col + g)),
            pl.BlockSpec((1, ns), lambda b, g, c: (0, cw_col + g)),
            pl.BlockSpec((1, gw), lambda b, g, c: (0, g)),
            pl.BlockSpec((1, gw), lambda b, g, c: (0, g)),
        ],
        out_specs=pl.BlockSpec((1, ln, gw), lambda b, g, c: (b, c, g)),
        out_shape=jax.ShapeDtypeStruct((bsz, s, SSM_GROUPS * gw), BF16),
        scratch_shapes=[
            pltpu.VMEM((ln + V7X_SUBLANES, gw), F32),
            pltpu.VMEM((ln + V7X_SUBLANES, ns), F32),
            pltpu.VMEM((ln + V7X_SUBLANES, ns), F32),
            pltpu.VMEM((ns, gw), F32),
            pltpu.VMEM((V7X_LANES, ln), F32),
            pltpu.VMEM((ln, gw), F32),
        ],
        compiler_params=_params("parallel", "parallel", "arbitrary"),
        name="ssd",
    )(dt_raw, dt_bias, a_log, proj, proj, proj, proj,
      conv_w, conv_w, conv_w, conv_b, conv_b, conv_b, d_skip_row, norm_g)


def _merge_kernel(a_ref, s_ref, wa_ref, ws_ref, ga_ref, gs_ref, out_ref):
    ya = jnp.dot(a_ref[...], wa_ref[...], preferred_element_type=F32)
    ys = jnp.dot(s_ref[...], ws_ref[...], preferred_element_type=F32)
    ga = _sigmoid(ga_ref[...].astype(F32))
    gs = _sigmoid(gs_ref[...].astype(F32))
    out_ref[...] = (ga * ya + gs * ys).astype(out_ref.dtype)


def _merge(o_att, o_ssm, w_att, w_ssm, proj, *, tm, tn, ga_col, gs_col):
    t, ka = o_att.shape
    ks = o_ssm.shape[1]
    n = w_att.shape[1]
    assert t % tm == 0 and n % tn == 0, (t, tm, n, tn)
    return pl.pallas_call(
        _merge_kernel,
        grid=(t // tm, n // tn),
        in_specs=[
            pl.BlockSpec((tm, ka), lambda i, j: (i, 0)),
            pl.BlockSpec((tm, ks), lambda i, j: (i, 0)),
            pl.BlockSpec((ka, tn), lambda i, j: (0, j)),
            pl.BlockSpec((ks, tn), lambda i, j: (0, j)),
            pl.BlockSpec((tm, tn), lambda i, j: (i, ga_col + j)),
            pl.BlockSpec((tm, tn), lambda i, j: (i, gs_col + j)),
        ],
        out_specs=pl.BlockSpec((tm, tn), lambda i, j: (i, j)),
        out_shape=jax.ShapeDtypeStruct((t, n), BF16),
        compiler_params=_params("parallel", "arbitrary"),
        name="merge",
    )(o_att, o_ssm, w_att, w_ssm, proj, proj)


def _outproj_kernel(m_ref, w_ref, h_ref, g_ref, out_ref):
    y = jnp.dot(m_ref[...], w_ref[...], preferred_element_type=F32)
    out_ref[...] = h_ref[...] + _rms_scale(y) * g_ref[...]


def _outproj(merged, w_out, h, post_g, *, tm):
    t, d = h.shape
    k = merged.shape[1]
    assert t % tm == 0, (t, tm)
    return pl.pallas_call(
        _outproj_kernel,
        grid=(t // tm,),
        in_specs=[
            pl.BlockSpec((tm, k), lambda i: (i, 0)),
            pl.BlockSpec((k, d), lambda i: (0, 0)),
            pl.BlockSpec((tm, d), lambda i: (i, 0)),
            pl.BlockSpec((1, d), lambda i: (0, 0)),
        ],
        out_specs=pl.BlockSpec((tm, d), lambda i: (i, 0)),
        out_shape=jax.ShapeDtypeStruct((t, d), F32),
        compiler_params=_params("parallel"),
        name="out_proj",
    )(merged, w_out, h, post_g)


def _pad_lanes(v):
    return jnp.pad(v.astype(F32), (0, V7X_LANES - v.shape[0]))[None, :]


def kernel(x, ffn1_pre_g, ffn1_w_gate, ffn1_w_up, ffn1_w_down, ffn1_post_g, mix_pre_g, w_in, att_lambda_q1, att_lambda_k1, att_lambda_q2, att_lambda_k2, att_subln_g, ssm_conv_w, ssm_conv_b, ssm_dt_bias, ssm_a_log, ssm_d, ssm_norm_g, w_branch_att, w_branch_ssm, w_out, mix_post_g, ffn2_pre_g, ffn2_w_gate, ffn2_w_up, ffn2_w_down, ffn2_post_g):
    bsz, seq, d = x.shape
    t = bsz * seq
    qk_w = ATT_HEADS * 2 * ATT_HEAD_DIM
    v_w = ATT_HEADS * ATT_V_DIM
    d_inner = ssm_norm_g.shape[1]
    n_heads = ssm_dt_bias.shape[1]
    bc_w = SSM_GROUPS * SSM_D_STATE
    conv_dim = d_inner + 2 * bc_w
    gw = d_inner // SSM_GROUPS

    w = w_in[0]
    o_q, o_k, o_v = 0, qk_w, 2 * qk_w
    o_z = o_v + v_w
    o_xbc = o_z + d_inner
    o_dt = o_xbc + conv_dim
    o_gate = o_dt + n_heads
    w_main = jnp.concatenate(
        [w[:, o_q:o_k] * (ATT_HEAD_DIM ** -0.5), w[:, o_k:o_dt], w[:, o_gate:]], axis=1).astype(BF16)
    w_dt = jnp.pad(w[:, o_dt:o_gate], ((0, 0), (0, V7X_LANES - n_heads))).astype(BF16)
    c_q, c_k, c_v, c_z, c_x = o_q, o_k, o_v, o_z, o_xbc
    c_b = c_x + d_inner
    c_c = c_b + bc_w
    c_ga = c_x + conv_dim
    c_gs = c_ga + d

    h = x.reshape(t, d)
    h = _ffn(h, ffn1_pre_g, ffn1_w_gate[0].astype(BF16), ffn1_w_up[0].astype(BF16),
             ffn1_w_down[0].astype(BF16), ffn1_post_g, tm=512, tf=512)

    proj, dt_raw = _inproj(h, mix_pre_g, w_main, w_dt, tm=1024, tn=1024)
    proj3 = proj.reshape(bsz, seq, proj.shape[1])

    o_att = _attention(proj3, att_lambda_q1, att_lambda_k1, att_lambda_q2, att_lambda_k2,
                       att_subln_g, tq=512,
                       q_col=c_q // ATT_V_DIM, k_col=c_k // ATT_V_DIM, v_col=c_v // ATT_V_DIM)

    d_skip_row = jnp.repeat(ssm_d[0].astype(F32), SSM_HEAD_DIM)[None, :]
    o_ssm = _ssd(proj3, dt_raw.reshape(bsz, seq, V7X_LANES), _pad_lanes(ssm_dt_bias[0]),
                 _pad_lanes(ssm_a_log[0]), ssm_conv_w[0], ssm_conv_b, d_skip_row, ssm_norm_g,
                 x_col=c_x // gw, b_col=c_b // SSM_D_STATE, c_col=c_c // SSM_D_STATE,
                 z_col=c_z // gw)

    tn_merge = 512
    merged = _merge(o_att.reshape(t, v_w), o_ssm.reshape(t, d_inner),
                    w_branch_att[0].astype(BF16), w_branch_ssm[0].astype(BF16), proj,
                    tm=1024, tn=tn_merge, ga_col=c_ga // tn_merge, gs_col=c_gs // tn_merge)
    h = _outproj(merged, w_out[0].astype(BF16), h, mix_post_g, tm=512)

    h = _ffn(h, ffn2_pre_g, ffn2_w_gate[0].astype(BF16), ffn2_w_up[0].astype(BF16),
             ffn2_w_down[0].astype(BF16), ffn2_post_g, tm=512, tf=512)
    return h.reshape(bsz, seq, d)
```

```python
import functools
import math

import jax
import jax.numpy as jnp
from jax import lax
from jax.experimental import pallas as pl
from jax.experimental.pallas import tpu as pltpu

F32 = jnp.float32
BF16 = jnp.bfloat16

V7X_SUBLANES = 8
V7X_LANES = 128
V7X_VMEM_LIMIT_BYTES = 56 * 1024 * 1024

RMS_EPS = 1e-6

ATT_HEADS = 8
ATT_HEAD_DIM = 128
ATT_V_DIM = 2 * ATT_HEAD_DIM
SSM_HEAD_DIM = 64
SSM_GROUPS = 8
SSM_D_STATE = 128
SSM_CONV = 4
SSM_CHUNK = 128
LAMBDA_INIT = 0.8 - 0.6 * math.exp(-0.3 * 0)
LOG2_E = math.log2(math.e)


def _params(*semantics):
    return pltpu.CompilerParams(dimension_semantics=semantics,
                                vmem_limit_bytes=V7X_VMEM_LIMIT_BYTES)


def _rms_scale(x):
    return x * lax.rsqrt(jnp.mean(x * x, axis=-1, keepdims=True) + RMS_EPS)


def _silu(x):
    return x / (1.0 + jnp.exp(-x))


def _sigmoid(x):
    return 1.0 / (1.0 + jnp.exp(-x))


def _ffn_kernel(h_ref, pre_g_ref, wg_ref, wu_ref, wd_ref, post_g_ref, out_ref, xn_ref, acc_ref):
    j = pl.program_id(1)

    @pl.when(j == 0)
    def _():
        xn_ref[...] = (_rms_scale(h_ref[...]) * pre_g_ref[...]).astype(BF16)

    xn = xn_ref[...]
    gate = jnp.dot(xn, wg_ref[...], preferred_element_type=F32)
    up = jnp.dot(xn, wu_ref[...], preferred_element_type=F32)
    act = (_silu(gate) * up).astype(BF16)
    part = jnp.dot(act, wd_ref[...], preferred_element_type=F32)

    @pl.when(j == 0)
    def _():
        acc_ref[...] = part

    @pl.when(j > 0)
    def _():
        acc_ref[...] += part

    @pl.when(j == pl.num_programs(1) - 1)
    def _():
        out_ref[...] = h_ref[...] + 0.5 * (_rms_scale(acc_ref[...]) * post_g_ref[...])


def _ffn(h, pre_g, w_gate, w_up, w_down, post_g, *, tm, tf):
    t, d = h.shape
    f = w_gate.shape[1]
    assert t % tm == 0 and f % tf == 0, (t, tm, f, tf)
    return pl.pallas_call(
        _ffn_kernel,
        grid=(t // tm, f // tf),
        in_specs=[
            pl.BlockSpec((tm, d), lambda i, j: (i, 0)),
            pl.BlockSpec((1, d), lambda i, j: (0, 0)),
            pl.BlockSpec((d, tf), lambda i, j: (0, j)),
            pl.BlockSpec((d, tf), lambda i, j: (0, j)),
            pl.BlockSpec((tf, d), lambda i, j: (j, 0)),
            pl.BlockSpec((1, d), lambda i, j: (0, 0)),
        ],
        out_specs=pl.BlockSpec((tm, d), lambda i, j: (i, 0)),
        out_shape=jax.ShapeDtypeStruct((t, d), F32),
        scratch_shapes=[pltpu.VMEM((tm, d), BF16), pltpu.VMEM((tm, d), F32)],
        compiler_params=_params("parallel", "arbitrary"),
        name="ffn",
    )(h, pre_g, w_gate, w_up, w_down, post_g)


def _inproj_kernel(h_ref, g_ref, w_ref, wdt_ref, out_ref, dt_ref, xn_ref):
    j = pl.program_id(1)

    @pl.when(j == 0)
    def _():
        xn = (_rms_scale(h_ref[...]) * g_ref[...]).astype(BF16)
        xn_ref[...] = xn
        dt_ref[...] = jnp.dot(xn, wdt_ref[...], preferred_element_type=F32)

    out_ref[...] = jnp.dot(xn_ref[...], w_ref[...], preferred_element_type=F32).astype(BF16)


def _inproj(h, g, w_main, w_dt, *, tm, tn):
    t, d = h.shape
    n = w_main.shape[1]
    ndt = w_dt.shape[1]
    assert t % tm == 0 and n % tn == 0, (t, tm, n, tn)
    return pl.pallas_call(
        _inproj_kernel,
        grid=(t // tm, n // tn),
        in_specs=[
            pl.BlockSpec((tm, d), lambda i, j: (i, 0)),
            pl.BlockSpec((1, d), lambda i, j: (0, 0)),
            pl.BlockSpec((d, tn), lambda i, j: (0, j)),
            pl.BlockSpec((d, ndt), lambda i, j: (0, 0)),
        ],
        out_specs=[
            pl.BlockSpec((tm, tn), lambda i, j: (i, j)),
            pl.BlockSpec((tm, ndt), lambda i, j: (i, 0)),
        ],
        out_shape=[jax.ShapeDtypeStruct((t, n), BF16), jax.ShapeDtypeStruct((t, ndt), F32)],
        scratch_shapes=[pltpu.VMEM((tm, d), BF16)],
        compiler_params=_params("parallel", "arbitrary"),
        name="in_proj",
    )(h, g, w_main, w_dt)


def _attn_kernel(lq1_ref, lk1_ref, lq2_ref, lk2_ref, g_ref, q_ref, k_ref, v_ref, o_ref,
                 vt_ref, m_ref, l_ref, acc_ref, sa_ref, sb_ref, *, tq):
    qi = pl.program_id(2)
    dh = ATT_HEAD_DIM
    n_kv = vt_ref.shape[0]

    @pl.when(qi == 0)
    def _():
        for j in range(n_kv):
            vt_ref[j] = v_ref[0, j * tq:(j + 1) * tq, :].T

    m_ref[...] = jnp.full(m_ref.shape, -jnp.inf, F32)
    l_ref[...] = jnp.zeros(l_ref.shape, F32)
    acc_ref[...] = jnp.zeros(acc_ref.shape, F32)

    def scores(j, s_ref):
        start = pl.multiple_of(j * tq, tq)
        for mi in range(2):
            q = q_ref[0, :, mi * dh:(mi + 1) * dh]
            k = k_ref[0, pl.ds(start, tq), mi * dh:(mi + 1) * dh]
            s_ref[mi] = lax.dot_general(k, q, (((1,), (1,)), ((), ())), preferred_element_type=F32)

    def softmax_pv(j, s_ref, masked):
        vt = vt_ref[j]
        for mi in range(2):
            st = s_ref[mi]
            if masked:
                kpos = lax.broadcasted_iota(jnp.int32, st.shape, 0)
                qpos = lax.broadcasted_iota(jnp.int32, st.shape, 1)
                st = jnp.where(kpos <= qpos, st, -jnp.inf)
            m_prev = m_ref[mi]
            m_new = jnp.maximum(m_prev, jnp.max(st, axis=0, keepdims=True))
            alpha = jnp.exp2(m_prev - m_new)
            p = jnp.exp2(st - m_new)
            l_ref[mi] = alpha * l_ref[mi] + jnp.sum(p, axis=0, keepdims=True)
            acc_ref[mi] = alpha * acc_ref[mi] + jnp.dot(vt, p.astype(BF16), preferred_element_type=F32)
            m_ref[mi] = m_new

    scores(0, sa_ref)

    def tile_pair(i, carry):
        j = 2 * i
        scores(j + 1, sb_ref)
        softmax_pv(j, sa_ref, masked=False)
        scores(j + 2, sa_ref)
        softmax_pv(j + 1, sb_ref, masked=False)
        return carry

    lax.fori_loop(0, qi // 2, tile_pair, 0)

    @pl.when(qi % 2 == 0)
    def _():
        softmax_pv(qi, sa_ref, masked=True)

    @pl.when(qi % 2 == 1)
    def _():
        scores(qi, sb_ref)
        softmax_pv(qi - 1, sa_ref, masked=False)
        softmax_pv(qi, sb_ref, masked=True)

    lam = (jnp.exp(jnp.sum(lq1_ref[...] * lk1_ref[...], axis=-1, keepdims=True))
           - jnp.exp(jnp.sum(lq2_ref[...] * lk2_ref[...], axis=-1, keepdims=True))
           + LAMBDA_INIT)
    ot = acc_ref[0] / l_ref[0] - lam * (acc_ref[1] / l_ref[1])
    ot = ot * lax.rsqrt(jnp.mean(ot * ot, axis=0, keepdims=True) + RMS_EPS)
    o_ref[0] = (ot.T * g_ref[...] * (1.0 - LAMBDA_INIT)).astype(o_ref.dtype)


def _attention(proj, lq1, lk1, lq2, lk2, subln_g, *, tq, q_col, k_col, v_col):
    b, s, _ = proj.shape
    w = ATT_V_DIM
    assert s % tq == 0, (s, tq)
    vec = pl.BlockSpec((1, ATT_HEAD_DIM), lambda bi, h, qi: (0, 0))
    return pl.pallas_call(
        functools.partial(_attn_kernel, tq=tq),
        grid=(b, ATT_HEADS, s // tq),
        in_specs=[
            vec, vec, vec, vec,
            pl.BlockSpec((1, w), lambda bi, h, qi: (0, 0)),
            pl.BlockSpec((1, tq, w), lambda bi, h, qi: (bi, qi, q_col + h)),
            pl.BlockSpec((1, s, w), lambda bi, h, qi: (bi, 0, k_col + h)),
            pl.BlockSpec((1, s, w), lambda bi, h, qi: (bi, 0, v_col + h)),
        ],
        out_specs=pl.BlockSpec((1, tq, w), lambda bi, h, qi: (bi, qi, h)),
        out_shape=jax.ShapeDtypeStruct((b, s, ATT_HEADS * w), BF16),
        scratch_shapes=[
            pltpu.VMEM((s // tq, w, tq), BF16),
            pltpu.VMEM((2, 1, tq), F32),
            pltpu.VMEM((2, 1, tq), F32),
            pltpu.VMEM((2, w, tq), F32),
            pltpu.VMEM((2, tq, tq), F32),
            pltpu.VMEM((2, tq, tq), F32),
        ],
        compiler_params=_params("parallel", "parallel", "arbitrary"),
        name="diff_attn",
    )(lq1, lk1, lq2, lk2, subln_g, proj, proj, proj)


def _split3(x):
    p1 = x.astype(BF16)
    r1 = x - p1.astype(F32)
    p2 = r1.astype(BF16)
    p3 = (r1 - p2.astype(F32)).astype(BF16)
    return p1, p2, p3


def _ssd_kernel(dt_ref, dtb_ref, alog_ref, x_ref, b_ref, c_ref, z_ref,
                wx_ref, wb_ref, wc_ref, bx_ref, bb_ref, bc_ref, dskip_ref, ng_ref,
                o_ref,
                xext_ref, bext_ref, cext_ref, state_ref, acst_ref, ydiag_ref):
    g = pl.program_id(1)
    c = pl.program_id(2)
    ln = SSM_CHUNK
    hp = SSM_HEAD_DIM
    heads = x_ref.shape[-1] // hp
    tail = V7X_SUBLANES

    @pl.when(c == 0)
    def _():
        xext_ref[0:tail, :] = jnp.zeros((tail, xext_ref.shape[1]), F32)
        bext_ref[0:tail, :] = jnp.zeros((tail, bext_ref.shape[1]), F32)
        cext_ref[0:tail, :] = jnp.zeros((tail, cext_ref.shape[1]), F32)
        state_ref[...] = jnp.zeros(state_ref.shape, F32)

    def conv_silu(ext_ref, blk_ref, w_ref, bias_ref):
        ext_ref[tail:tail + ln, :] = blk_ref[0].astype(F32)
        y = bias_ref[...]
        for k in range(SSM_CONV):
            y = y + w_ref[k:k + 1, :] * ext_ref[pl.ds(tail - (SSM_CONV - 1) + k, ln), :]
        ext_ref[0:tail, :] = ext_ref[ln:ln + tail, :]
        return _silu(y)

    xc = conv_silu(xext_ref, x_ref, wx_ref, bx_ref)
    bc = conv_silu(bext_ref, b_ref, wb_ref, bb_ref)
    cc = conv_silu(cext_ref, c_ref, wc_ref, bc_ref)

    xb = dt_ref[0] + dtb_ref[...]
    dt_all = jnp.maximum(xb, 0.0) + jnp.log1p(jnp.exp(-jnp.abs(xb)))
    adt = dt_all * (-jnp.exp(alog_ref[...]))
    row = lax.broadcasted_iota(jnp.int32, (ln, ln), 0)
    col = lax.broadcasted_iota(jnp.int32, (ln, ln), 1)
    tril = col <= row
    ltri = jnp.where(tril, 1.0, 0.0).astype(BF16)
    acs_all = sum(jnp.dot(ltri, part, preferred_element_type=F32) for part in _split3(adt))
    acst_ref[...] = acs_all.T

    shift = lax.rem(V7X_LANES - heads * g, V7X_LANES)
    dt_g = pltpu.roll(dt_all, shift, 1)
    acs_g = pltpu.roll(acs_all, shift, 1)

    lane = lax.broadcasted_iota(jnp.int32, (ln, 2 * hp), 1)
    lo = lane < hp
    cb = lax.dot_general(cc.astype(BF16), bc.astype(BF16), (((1,), (1,)), ((), ())),
                         preferred_element_type=F32)
    state = state_ref[...]
    y_off = jnp.dot(cc.astype(BF16), state.astype(BF16), preferred_element_type=F32)

    xw_parts = []
    dec_last_parts = []
    for pr in range(heads // 2):
        e0, e1 = 2 * pr, 2 * pr + 1
        sl = slice(pr * 2 * hp, (pr + 1) * 2 * hp)
        dt_pair = jnp.where(lo, dt_g[:, e0:e0 + 1], dt_g[:, e1:e1 + 1])
        acs_pair = jnp.where(lo, acs_g[:, e0:e0 + 1], acs_g[:, e1:e1 + 1])
        xdt = xc[:, sl] * dt_pair
        y_pair = y_off[:, sl] * jnp.exp(acs_pair)
        for e, keep in ((e0, lo), (e1, jnp.logical_not(lo))):
            rowv = acst_ref[pl.ds(heads * g + e, 1), :]
            colv = acs_g[:, e:e + 1]
            decay = jnp.exp(jnp.where(tril, colv - rowv, -jnp.inf))
            m = (cb * decay).astype(BF16)
            y_pair = y_pair + jnp.dot(m, jnp.where(keep, xdt, 0.0).astype(BF16),
                                      preferred_element_type=F32)
        ydiag_ref[:, sl] = y_pair
        last = acs_pair[ln - 1:ln, :]
        xw_parts.append((xdt * jnp.exp(last - acs_pair)).astype(BF16))
        dec_last_parts.append(jnp.exp(last))

    xw = jnp.concatenate(xw_parts, axis=1)
    dec_last = jnp.concatenate(dec_last_parts, axis=1)
    state_ref[...] = state * dec_last + jnp.dot(bc.T.astype(BF16), xw, preferred_element_type=F32)

    y = ydiag_ref[...] + xc * dskip_ref[...]
    zz = z_ref[0].astype(F32)
    y = y * _silu(zz)
    o_ref[0] = (_rms_scale(y) * ng_ref[...]).astype(o_ref.dtype)


def _ssd(proj, dt_raw, dt_bias, a_log, conv_w, conv_b, d_skip_row, norm_g, *,
         x_col, b_col, c_col, z_col):
    bsz, s, _ = proj.shape
    ln = SSM_CHUNK
    gw = d_skip_row.shape[1] // SSM_GROUPS
    ns = SSM_D_STATE
    assert s % ln == 0 and gw % (2 * SSM_HEAD_DIM) == 0, (s, ln, gw)
    xw_col = 0
    bw_col = (SSM_GROUPS * gw) // ns
    cw_col = bw_col + SSM_GROUPS
    row128 = pl.BlockSpec((1, V7X_LANES), lambda b, g, c: (0, 0))
    return pl.pallas_call(
        _ssd_kernel,
        grid=(bsz, SSM_GROUPS, s // ln),
        in_specs=[
            pl.BlockSpec((1, ln, V7X_LANES), lambda b, g, c: (b, c, 0)),
            row128, row128,
            pl.BlockSpec((1, ln, gw), lambda b, g, c: (b, c, x_col + g)),
            pl.BlockSpec((1, ln, ns), lambda b, g, c: (b, c, b_col + g)),
            pl.BlockSpec((1, ln, ns), lambda b, g, c: (b, c, c_col + g)),
            pl.BlockSpec((1, ln, gw), lambda b, g, c: (b, c, z_col + g)),
            pl.BlockSpec((SSM_CONV, gw), lambda b, g, c: (0, xw_col + g)),
            pl.BlockSpec((SSM_CONV, ns), lambda b, g, c: (0, bw_col + g)),
            pl.BlockSpec((SSM_CONV, ns), lambda b, g, c: (0, cw_col + g)),
            pl.BlockSpec((1, gw), lambda b, g, c: (0, xw_col + g)),
            pl.BlockSpec((1, ns), lambda b, g, c: (0, bw_col + g)),
            pl.BlockSpec((1, ns), lambda b, g, c: (0, cw_col + g)),
            pl.BlockSpec((1, gw), lambda b, g, c: (0, g)),
            pl.BlockSpec((1, gw), lambda b, g, c: (0, g)),
        ],
        out_specs=pl.BlockSpec((1, ln, gw), lambda b, g, c: (b, c, g)),
        out_shape=jax.ShapeDtypeStruct((bsz, s, SSM_GROUPS * gw), BF16),
        scratch_shapes=[
            pltpu.VMEM((ln + V7X_SUBLANES, gw), F32),
            pltpu.VMEM((ln + V7X_SUBLANES, ns), F32),
            pltpu.VMEM((ln + V7X_SUBLANES, ns), F32),
            pltpu.VMEM((ns, gw), F32),
            pltpu.VMEM((V7X_LANES, ln), F32),
            pltpu.VMEM((ln, gw), F32),
        ],
        compiler_params=_params("parallel", "parallel", "arbitrary"),
        name="ssd",
    )(dt_raw, dt_bias, a_log, proj, proj, proj, proj,
      conv_w, conv_w, conv_w, conv_b, conv_b, conv_b, d_skip_row, norm_g)


def _merge_kernel(a_ref, s_ref, wa_ref, ws_ref, ga_ref, gs_ref, out_ref):
    ya = jnp.dot(a_ref[...], wa_ref[...], preferred_element_type=F32)
    ys = jnp.dot(s_ref[...], ws_ref[...], preferred_element_type=F32)
    ga = _sigmoid(ga_ref[...].astype(F32))
    gs = _sigmoid(gs_ref[...].astype(F32))
    out_ref[...] = (ga * ya + gs * ys).astype(out_ref.dtype)


def _merge(o_att, o_ssm, w_att, w_ssm, proj, *, tm, tn, ga_col, gs_col):
    t, ka = o_att.shape
    ks = o_ssm.shape[1]
    n = w_att.shape[1]
    assert t % tm == 0 and n % tn == 0, (t, tm, n, tn)
    return pl.pallas_call(
        _merge_kernel,
        grid=(t // tm, n // tn),
        in_specs=[
            pl.BlockSpec((tm, ka), lambda i, j: (i, 0)),
            pl.BlockSpec((tm, ks), lambda i, j: (i, 0)),
            pl.BlockSpec((ka, tn), lambda i, j: (0, j)),
            pl.BlockSpec((ks, tn), lambda i, j: (0, j)),
            pl.BlockSpec((tm, tn), lambda i, j: (i, ga_col + j)),
            pl.BlockSpec((tm, tn), lambda i, j: (i, gs_col + j)),
        ],
        out_specs=pl.BlockSpec((tm, tn), lambda i, j: (i, j)),
        out_shape=jax.ShapeDtypeStruct((t, n), BF16),
        compiler_params=_params("parallel", "arbitrary"),
        name="merge",
    )(o_att, o_ssm, w_att, w_ssm, proj, proj)


def _outproj_kernel(m_ref, w_ref, h_ref, g_ref, out_ref):
    y = jnp.dot(m_ref[...], w_ref[...], preferred_element_type=F32)
    out_ref[...] = h_ref[...] + _rms_scale(y) * g_ref[...]


def _outproj(merged, w_out, h, post_g, *, tm):
    t, d = h.shape
    k = merged.shape[1]
    assert t % tm == 0, (t, tm)
    return pl.pallas_call(
        _outproj_kernel,
        grid=(t // tm,),
        in_specs=[
            pl.BlockSpec((tm, k), lambda i: (i, 0)),
            pl.BlockSpec((k, d), lambda i: (0, 0)),
            pl.BlockSpec((tm, d), lambda i: (i, 0)),
            pl.BlockSpec((1, d), lambda i: (0, 0)),
        ],
        out_specs=pl.BlockSpec((tm, d), lambda i: (i, 0)),
        out_shape=jax.ShapeDtypeStruct((t, d), F32),
        compiler_params=_params("parallel"),
        name="out_proj",
    )(merged, w_out, h, post_g)


def _pad_lanes(v):
    return jnp.pad(v.astype(F32), (0, V7X_LANES - v.shape[0]))[None, :]


def kernel(x, ffn1_pre_g, ffn1_w_gate, ffn1_w_up, ffn1_w_down, ffn1_post_g, mix_pre_g, w_in, att_lambda_q1, att_lambda_k1, att_lambda_q2, att_lambda_k2, att_subln_g, ssm_conv_w, ssm_conv_b, ssm_dt_bias, ssm_a_log, ssm_d, ssm_norm_g, w_branch_att, w_branch_ssm, w_out, mix_post_g, ffn2_pre_g, ffn2_w_gate, ffn2_w_up, ffn2_w_down, ffn2_post_g):
    bsz, seq, d = x.shape
    t = bsz * seq
    qk_w = ATT_HEADS * 2 * ATT_HEAD_DIM
    v_w = ATT_HEADS * ATT_V_DIM
    d_inner = ssm_norm_g.shape[1]
    n_heads = ssm_dt_bias.shape[1]
    bc_w = SSM_GROUPS * SSM_D_STATE
    conv_dim = d_inner + 2 * bc_w
    gw = d_inner // SSM_GROUPS

    w = w_in[0]
    o_q, o_k, o_v = 0, qk_w, 2 * qk_w
    o_z = o_v + v_w
    o_xbc = o_z + d_inner
    o_dt = o_xbc + conv_dim
    o_gate = o_dt + n_heads
    w_main = jnp.concatenate(
        [w[:, o_q:o_k] * (LOG2_E * ATT_HEAD_DIM ** -0.5), w[:, o_k:o_dt], w[:, o_gate:]], axis=1).astype(BF16)
    w_dt = jnp.pad(w[:, o_dt:o_gate], ((0, 0), (0, V7X_LANES - n_heads))).astype(BF16)
    c_q, c_k, c_v, c_z, c_x = o_q, o_k, o_v, o_z, o_xbc
    c_b = c_x + d_inner
    c_c = c_b + bc_w
    c_ga = c_x + conv_dim
    c_gs = c_ga + d

    h = x.reshape(t, d)
    h = _ffn(h, ffn1_pre_g, ffn1_w_gate[0].astype(BF16), ffn1_w_up[0].astype(BF16),
             ffn1_w_down[0].astype(BF16), ffn1_post_g, tm=512, tf=512)

    proj, dt_raw = _inproj(h, mix_pre_g, w_main, w_dt, tm=1024, tn=1024)
    proj3 = proj.reshape(bsz, seq, proj.shape[1])

    o_att = _attention(proj3, att_lambda_q1, att_lambda_k1, att_lambda_q2, att_lambda_k2,
                       att_subln_g, tq=512,
                       q_col=c_q // ATT_V_DIM, k_col=c_k // ATT_V_DIM, v_col=c_v // ATT_V_DIM)

    d_skip_row = jnp.repeat(ssm_d[0].astype(F32), SSM_HEAD_DIM)[None, :]
    o_ssm = _ssd(proj3, dt_raw.reshape(bsz, seq, V7X_LANES), _pad_lanes(ssm_dt_bias[0]),
                 _pad_lanes(ssm_a_log[0]), ssm_conv_w[0], ssm_conv_b, d_skip_row, ssm_norm_g,
                 x_col=c_x // gw, b_col=c_b // SSM_D_STATE, c_col=c_c // SSM_D_STATE,
                 z_col=c_z // gw)

    tn_merge = 512
    merged = _merge(o_att.reshape(t, v_w), o_ssm.reshape(t, d_inner),
                    w_branch_att[0].astype(BF16), w_branch_ssm[0].astype(BF16), proj,
                    tm=1024, tn=tn_merge, ga_col=c_ga // tn_merge, gs_col=c_gs // tn_merge)
    h = _outproj(merged, w_out[0].astype(BF16), h, mix_post_g, tm=512)

    h = _ffn(h, ffn2_pre_g, ffn2_w_gate[0].astype(BF16), ffn2_w_up[0].astype(BF16),
             ffn2_w_down[0].astype(BF16), ffn2_post_g, tm=512, tf=512)
    return h.reshape(bsz, seq, d)
```

```python
import functools
import math

import jax
import jax.numpy as jnp
from jax import lax
from jax.experimental import pallas as pl
from jax.experimental.pallas import tpu as pltpu

F32 = jnp.float32
BF16 = jnp.bfloat16

V7X_SUBLANES = 8
V7X_LANES = 128
V7X_VMEM_LIMIT_BYTES = 56 * 1024 * 1024

RMS_EPS = 1e-6

ATT_HEADS = 8
ATT_HEAD_DIM = 128
ATT_V_DIM = 2 * ATT_HEAD_DIM
SSM_HEAD_DIM = 64
SSM_GROUPS = 8
SSM_D_STATE = 128
SSM_CONV = 4
SSM_CHUNK = 128
LAMBDA_INIT = 0.8 - 0.6 * math.exp(-0.3 * 0)
LOG2_E = math.log2(math.e)

FFN_DOWN_CHUNK = 512
INPROJ_CHUNK = 256
INPROJ_CONV_ROWS = 128


def _params(*semantics):
    return pltpu.CompilerParams(dimension_semantics=semantics,
                                vmem_limit_bytes=V7X_VMEM_LIMIT_BYTES)


def _rms_scale(x):
    return x * lax.rsqrt(jnp.mean(x * x, axis=-1, keepdims=True) + RMS_EPS)


def _silu(x):
    return x / (1.0 + jnp.exp(-x))


def _sigmoid(x):
    return 1.0 / (1.0 + jnp.exp(-x))


def _ffn_kernel(h_ref, pre_g_ref, wg_ref, wu_ref, wd_ref, post_g_ref, out_ref, xn_ref, acc_ref):
    j = pl.program_id(1)

    @pl.when(j == 0)
    def _():
        xn_ref[...] = (_rms_scale(h_ref[...]) * pre_g_ref[...]).astype(BF16)
        acc_ref[...] = jnp.zeros(acc_ref.shape, F32)

    xn = xn_ref[...]
    gate = jnp.dot(xn, wg_ref[...], preferred_element_type=F32)
    up = jnp.dot(xn, wu_ref[...], preferred_element_type=F32)
    act = (_silu(gate) * up).astype(BF16)
    d = acc_ref.shape[1]
    for c in range(d // FFN_DOWN_CHUNK):
        cols = slice(c * FFN_DOWN_CHUNK, (c + 1) * FFN_DOWN_CHUNK)
        acc_ref[:, cols] += jnp.dot(act, wd_ref[:, cols], preferred_element_type=F32)

    @pl.when(j == pl.num_programs(1) - 1)
    def _():
        out_ref[...] = h_ref[...] + 0.5 * (_rms_scale(acc_ref[...]) * post_g_ref[...])


def _ffn(h, pre_g, w_gate, w_up, w_down, post_g, *, tm, tf):
    t, d = h.shape
    f = w_gate.shape[1]
    assert t % tm == 0 and f % tf == 0, (t, tm, f, tf)
    return pl.pallas_call(
        _ffn_kernel,
        grid=(t // tm, f // tf),
        in_specs=[
            pl.BlockSpec((tm, d), lambda i, j: (i, 0)),
            pl.BlockSpec((1, d), lambda i, j: (0, 0)),
            pl.BlockSpec((d, tf), lambda i, j: (0, j)),
            pl.BlockSpec((d, tf), lambda i, j: (0, j)),
            pl.BlockSpec((tf, d), lambda i, j: (j, 0)),
            pl.BlockSpec((1, d), lambda i, j: (0, 0)),
        ],
        out_specs=pl.BlockSpec((tm, d), lambda i, j: (i, 0)),
        out_shape=jax.ShapeDtypeStruct((t, d), F32),
        scratch_shapes=[pltpu.VMEM((tm, d), BF16), pltpu.VMEM((tm, d), F32)],
        compiler_params=_params("parallel", "arbitrary"),
        name="ffn",
    )(h, pre_g, w_gate, w_up, w_down, post_g)


def _inproj_kernel(h_ref, g_ref, wa_ref, wg_ref, wdt_ref, cw_ref, cb_ref, out_ref, dt_ref,
                   xn_ref, ext_ref, tail_ref, *, j_k, j_z, j_x, j_g, tiles_per_seq):
    i = pl.program_id(0)
    j = pl.program_id(1)
    tm, tn = out_ref.shape
    tail = V7X_SUBLANES

    @pl.when(j == 0)
    def _():
        xn = (_rms_scale(h_ref[...]) * g_ref[...]).astype(BF16)
        xn_ref[...] = xn
        dt_ref[...] = jnp.dot(xn, wdt_ref[...], preferred_element_type=F32)

        @pl.when(i == 0)
        def _():
            tail_ref[...] = jnp.zeros(tail_ref.shape, F32)

    def project(w_ref, epilogue):
        xn = xn_ref[...]
        pending = None
        for c in range(tn // INPROJ_CHUNK):
            cols = slice(c * INPROJ_CHUNK, (c + 1) * INPROJ_CHUNK)
            y = jnp.dot(xn, w_ref[:, cols], preferred_element_type=F32)
            if pending is not None:
                epilogue(*pending)
            pending = (y, cols)
        epilogue(*pending)

    def elementwise(fn):
        def epilogue(y, cols):
            out_ref[:, cols] = fn(y).astype(out_ref.dtype)
        return epilogue

    def conv_silu(y, cols):
        jx = j - j_x
        ext_ref[tail:tail + tm, cols] = y
        ext_ref[0:tail, cols] = tail_ref[jx, :, cols]
        for r in range(tm // INPROJ_CONV_ROWS):
            r0 = r * INPROJ_CONV_ROWS
            acc = cb_ref[:, cols]
            for k in range(SSM_CONV):
                acc = acc + cw_ref[k:k + 1, cols] * ext_ref[
                    pl.ds(r0 + tail - (SSM_CONV - 1) + k, INPROJ_CONV_ROWS), cols]
            out_ref[r0:r0 + INPROJ_CONV_ROWS, cols] = _silu(acc).astype(out_ref.dtype)
        seq_ends = (i % tiles_per_seq) == tiles_per_seq - 1
        tail_ref[jx, :, cols] = jnp.where(seq_ends, 0.0, ext_ref[tm:tm + tail, cols])

    @pl.when(j < j_k)
    def _():
        project(wa_ref, elementwise(lambda y: y * (LOG2_E * ATT_HEAD_DIM ** -0.5)))

    @pl.when(jnp.logical_and(j >= j_k, j < j_z))
    def _():
        project(wa_ref, elementwise(lambda y: y))

    @pl.when(jnp.logical_and(j >= j_z, j < j_x))
    def _():
        project(wa_ref, elementwise(_silu))

    @pl.when(jnp.logical_and(j >= j_x, j < j_g))
    def _():
        project(wa_ref, conv_silu)

    @pl.when(j >= j_g)
    def _():
        project(wg_ref, elementwise(_sigmoid))


def _inproj(h, g, w_a, w_g, w_dt, conv_w, conv_b, *, tm, tn, seq, qk_w, kv_end, z_end):
    t, d = h.shape
    na, ng = w_a.shape[1], w_g.shape[1]
    ndt = w_dt.shape[1]
    conv_dim = conv_w.shape[1]
    assert t % tm == 0 and seq % tm == 0, (t, seq, tm)
    assert all(v % tn == 0 for v in (qk_w, kv_end, z_end, na, ng, conv_dim)), (tn,)
    assert z_end + conv_dim == na and tn % INPROJ_CHUNK == 0
    j_k, j_z, j_x, j_g = qk_w // tn, kv_end // tn, z_end // tn, na // tn
    n_conv = conv_dim // tn
    return pl.pallas_call(
        functools.partial(_inproj_kernel, j_k=j_k, j_z=j_z, j_x=j_x, j_g=j_g,
                          tiles_per_seq=seq // tm),
        grid=(t // tm, (na + ng) // tn),
        in_specs=[
            pl.BlockSpec((tm, d), lambda i, j: (i, 0)),
            pl.BlockSpec((1, d), lambda i, j: (0, 0)),
            pl.BlockSpec((d, tn), lambda i, j: (0, jnp.minimum(j, j_g - 1))),
            pl.BlockSpec((d, tn), lambda i, j: (0, jnp.maximum(j - j_g, 0))),
            pl.BlockSpec((d, ndt), lambda i, j: (0, 0)),
            pl.BlockSpec((SSM_CONV, tn), lambda i, j: (0, jnp.clip(j - j_x, 0, n_conv - 1))),
            pl.BlockSpec((1, tn), lambda i, j: (0, jnp.clip(j - j_x, 0, n_conv - 1))),
        ],
        out_specs=[
            pl.BlockSpec((tm, tn), lambda i, j: (i, j)),
            pl.BlockSpec((tm, ndt), lambda i, j: (i, 0)),
        ],
        out_shape=[jax.ShapeDtypeStruct((t, na + ng), BF16), jax.ShapeDtypeStruct((t, ndt), F32)],
        scratch_shapes=[
            pltpu.VMEM((tm, d), BF16),
            pltpu.VMEM((tm + V7X_SUBLANES, tn), F32),
            pltpu.VMEM((n_conv, V7X_SUBLANES, tn), F32),
        ],
        compiler_params=_params("arbitrary", "arbitrary"),
        name="in_proj",
    )(h, g, w_a, w_g, w_dt, conv_w, conv_b)


def _attn_kernel(lq1_ref, lk1_ref, lq2_ref, lk2_ref, g_ref, q_ref, k_ref, v_ref, o_ref,
                 vt_ref, m_ref, l_ref, acc_ref, sa_ref, sb_ref, *, tq):
    qi = pl.program_id(2)
    dh = ATT_HEAD_DIM
    n_kv = vt_ref.shape[0]

    @pl.when(qi == 0)
    def _():
        for j in range(n_kv):
            vt_ref[j] = v_ref[0, j * tq:(j + 1) * tq, :].T

    m_ref[...] = jnp.full(m_ref.shape, -jnp.inf, F32)
    l_ref[...] = jnp.zeros(l_ref.shape, F32)
    acc_ref[...] = jnp.zeros(acc_ref.shape, F32)

    def scores(j, s_ref):
        start = pl.multiple_of(j * tq, tq)
        for mi in range(2):
            q = q_ref[0, :, mi * dh:(mi + 1) * dh]
            k = k_ref[0, pl.ds(start, tq), mi * dh:(mi + 1) * dh]
            s_ref[mi] = lax.dot_general(k, q, (((1,), (1,)), ((), ())), preferred_element_type=F32)

    def softmax_pv(j, s_ref, masked):
        vt = vt_ref[j]
        for mi in range(2):
            st = s_ref[mi]
            if masked:
                kpos = lax.broadcasted_iota(jnp.int32, st.shape, 0)
                qpos = lax.broadcasted_iota(jnp.int32, st.shape, 1)
                st = jnp.where(kpos <= qpos, st, -jnp.inf)
            m_prev = m_ref[mi]
            m_new = jnp.maximum(m_prev, jnp.max(st, axis=0, keepdims=True))
            alpha = jnp.exp2(m_prev - m_new)
            p = jnp.exp2(st - m_new)
            l_ref[mi] = alpha * l_ref[mi] + jnp.sum(p, axis=0, keepdims=True)
            acc_ref[mi] = alpha * acc_ref[mi] + jnp.dot(vt, p.astype(BF16), preferred_element_type=F32)
            m_ref[mi] = m_new

    scores(0, sa_ref)

    def tile_pair(i, carry):
        j = 2 * i
        scores(j + 1, sb_ref)
        softmax_pv(j, sa_ref, masked=False)
        scores(j + 2, sa_ref)
        softmax_pv(j + 1, sb_ref, masked=False)
        return carry

    lax.fori_loop(0, qi // 2, tile_pair, 0)

    @pl.when(qi % 2 == 0)
    def _():
        softmax_pv(qi, sa_ref, masked=True)

    @pl.when(qi % 2 == 1)
    def _():
        scores(qi, sb_ref)
        softmax_pv(qi - 1, sa_ref, masked=False)
        softmax_pv(qi, sb_ref, masked=True)

    lam = (jnp.exp(jnp.sum(lq1_ref[...] * lk1_ref[...], axis=-1, keepdims=True))
           - jnp.exp(jnp.sum(lq2_ref[...] * lk2_ref[...], axis=-1, keepdims=True))
           + LAMBDA_INIT)
    ot = acc_ref[0] / l_ref[0] - lam * (acc_ref[1] / l_ref[1])
    ot = ot * lax.rsqrt(jnp.mean(ot * ot, axis=0, keepdims=True) + RMS_EPS)
    o_ref[0] = (ot.T * g_ref[...] * (1.0 - LAMBDA_INIT)).astype(o_ref.dtype)


def _attention(proj, lq1, lk1, lq2, lk2, subln_g, *, tq, q_col, k_col, v_col):
    b, s, _ = proj.shape
    w = ATT_V_DIM
    assert s % tq == 0, (s, tq)
    vec = pl.BlockSpec((1, ATT_HEAD_DIM), lambda bi, h, qi: (0, 0))
    return pl.pallas_call(
        functools.partial(_attn_kernel, tq=tq),
        grid=(b, ATT_HEADS, s // tq),
        in_specs=[
            vec, vec, vec, vec,
            pl.BlockSpec((1, w), lambda bi, h, qi: (0, 0)),
            pl.BlockSpec((1, tq, w), lambda bi, h, qi: (bi, qi, q_col + h)),
            pl.BlockSpec((1, s, w), lambda bi, h, qi: (bi, 0, k_col + h)),
            pl.BlockSpec((1, s, w), lambda bi, h, qi: (bi, 0, v_col + h)),
        ],
        out_specs=pl.BlockSpec((1, tq, w), lambda bi, h, qi: (bi, qi, h)),
        out_shape=jax.ShapeDtypeStruct((b, s, ATT_HEADS * w), BF16),
        scratch_shapes=[
            pltpu.VMEM((s // tq, w, tq), BF16),
            pltpu.VMEM((2, 1, tq), F32),
            pltpu.VMEM((2, 1, tq), F32),
            pltpu.VMEM((2, w, tq), F32),
            pltpu.VMEM((2, tq, tq), F32),
            pltpu.VMEM((2, tq, tq), F32),
        ],
        compiler_params=_params("parallel", "parallel", "arbitrary"),
        name="diff_attn",
    )(lq1, lk1, lq2, lk2, subln_g, proj, proj, proj)


def _split3(x):
    p1 = x.astype(BF16)
    r1 = x - p1.astype(F32)
    p2 = r1.astype(BF16)
    p3 = (r1 - p2.astype(F32)).astype(BF16)
    return p1, p2, p3


def _ssd_kernel(dt_ref, dtb_ref, alog_ref, x_ref, b_ref, c_ref, z_ref, dskip_ref, ng_ref,
                o_ref,
                state_ref, dt_all_ref, acs_ref, acst_ref):
    c = pl.program_id(1)
    gblk = pl.program_id(2)
    ln = SSM_CHUNK
    hp = SSM_HEAD_DIM
    ns = SSM_D_STATE
    n_grp = b_ref.shape[-1] // ns
    gw = x_ref.shape[-1] // n_grp
    heads = gw // hp

    row = lax.broadcasted_iota(jnp.int32, (ln, ln), 0)
    col = lax.broadcasted_iota(jnp.int32, (ln, ln), 1)
    tril = col <= row

    @pl.when(gblk == 0)
    def _():
        xb = dt_ref[0] + dtb_ref[...]
        dt_all = jnp.maximum(xb, 0.0) + jnp.log1p(jnp.exp(-jnp.abs(xb)))
        adt = dt_all * (-jnp.exp(alog_ref[...]))
        ltri = jnp.where(tril, 1.0, 0.0).astype(BF16)
        acs_all = sum(jnp.dot(ltri, part, preferred_element_type=F32) for part in _split3(adt))
        dt_all_ref[...] = dt_all
        acs_ref[...] = acs_all
        acst_ref[...] = acs_all.T

    shift = lax.rem(V7X_LANES - n_grp * heads * gblk, V7X_LANES)
    dt_blk = pltpu.roll(dt_all_ref[...], shift, 1)
    acs_blk = pltpu.roll(acs_ref[...], shift, 1)

    lane = lax.broadcasted_iota(jnp.int32, (ln, 2 * hp), 1)
    lo = lane < hp
    hi = jnp.logical_not(lo)

    for gi in range(n_grp):
        g = gblk * n_grp + gi
        gcols = slice(gi * gw, (gi + 1) * gw)
        scols = slice(gi * ns, (gi + 1) * ns)

        @pl.when(c == 0)
        def _():
            state_ref[g] = jnp.zeros(state_ref.shape[1:], F32)

        xc = x_ref[0, :, gcols].astype(F32)
        bc = b_ref[0, :, scols]
        cc = c_ref[0, :, scols]
        cb = lax.dot_general(cc, bc, (((1,), (1,)), ((), ())), preferred_element_type=F32)
        state = state_ref[g]
        y_off = jnp.dot(cc, state.astype(BF16), preferred_element_type=F32)

        y_parts = []
        xw_parts = []
        dec_last_parts = []
        for pr in range(heads // 2):
            sl = slice(pr * 2 * hp, (pr + 1) * 2 * hp)
            h0 = gi * heads + 2 * pr
            acs_b = [jnp.broadcast_to(acs_blk[:, h:h + 1], (ln, 2 * hp)) for h in (h0, h0 + 1)]
            dt_pair = jnp.where(lo, dt_blk[:, h0:h0 + 1], dt_blk[:, h0 + 1:h0 + 2])
            acs_pair = jnp.where(lo, acs_b[0], acs_b[1])
            xdt = xc[:, sl] * dt_pair
            y_pair = y_off[:, sl] * jnp.exp(acs_pair)
            for k, keep in enumerate((lo, hi)):
                rowv = acst_ref[pl.ds(g * heads + 2 * pr + k, 1), :]
                decay = jnp.exp(jnp.where(tril, acs_b[k] - rowv, -jnp.inf))
                m = (cb * decay).astype(BF16)
                y_pair = y_pair + jnp.dot(m, jnp.where(keep, xdt, 0.0).astype(BF16),
                                          preferred_element_type=F32)
            y_parts.append(y_pair)
            last = acs_pair[ln - 1:ln, :]
            xw_parts.append((xdt * jnp.exp(last - acs_pair)).astype(BF16))
            dec_last_parts.append(jnp.exp(last))

        xw = jnp.concatenate(xw_parts, axis=1)
        dec_last = jnp.concatenate(dec_last_parts, axis=1)
        bct = bc.astype(F32).T.astype(BF16)
        state_ref[g] = state * dec_last + jnp.dot(bct, xw, preferred_element_type=F32)

        y = jnp.concatenate(y_parts, axis=1)
        y = (y + xc * dskip_ref[:, gcols]) * z_ref[0, :, gcols].astype(F32)
        o_ref[0, :, gcols] = (_rms_scale(y) * ng_ref[:, gcols]).astype(o_ref.dtype)


def _ssd(proj, dt_raw, dt_bias, a_log, d_skip_row, norm_g, *, n_grp, x_col, b_col, c_col, z_col):
    bsz, s, _ = proj.shape
    ln = SSM_CHUNK
    gw = d_skip_row.shape[1] // SSM_GROUPS
    xw, sw = n_grp * gw, n_grp * SSM_D_STATE
    assert s % ln == 0 and gw % (2 * SSM_HEAD_DIM) == 0 and SSM_GROUPS % n_grp == 0, (s, ln, gw)
    assert x_col % xw == 0 and z_col % xw == 0 and b_col % sw == 0 and c_col % sw == 0
    row128 = pl.BlockSpec((1, V7X_LANES), lambda b, c, g: (0, 0))
    return pl.pallas_call(
        _ssd_kernel,
        grid=(bsz, s // ln, SSM_GROUPS // n_grp),
        in_specs=[
            pl.BlockSpec((1, ln, V7X_LANES), lambda b, c, g: (b, c, 0)),
            row128, row128,
            pl.BlockSpec((1, ln, xw), lambda b, c, g: (b, c, x_col // xw + g)),
            pl.BlockSpec((1, ln, sw), lambda b, c, g: (b, c, b_col // sw + g)),
            pl.BlockSpec((1, ln, sw), lambda b, c, g: (b, c, c_col // sw + g)),
            pl.BlockSpec((1, ln, xw), lambda b, c, g: (b, c, z_col // xw + g)),
            pl.BlockSpec((1, xw), lambda b, c, g: (0, g)),
            pl.BlockSpec((1, xw), lambda b, c, g: (0, g)),
        ],
        out_specs=pl.BlockSpec((1, ln, xw), lambda b, c, g: (b, c, g)),
        out_shape=jax.ShapeDtypeStruct((bsz, s, SSM_GROUPS * gw), BF16),
        scratch_shapes=[
            pltpu.VMEM((SSM_GROUPS, SSM_D_STATE, gw), F32),
            pltpu.VMEM((ln, V7X_LANES), F32),
            pltpu.VMEM((ln, V7X_LANES), F32),
            pltpu.VMEM((V7X_LANES, ln), F32),
        ],
        compiler_params=_params("parallel", "arbitrary", "arbitrary"),
        name="ssd",
    )(dt_raw, dt_bias, a_log, proj, proj, proj, proj, d_skip_row, norm_g)


def _merge_kernel(a_ref, s_ref, wa_ref, ws_ref, ga_ref, gs_ref, out_ref):
    ya = jnp.dot(a_ref[...], wa_ref[...], preferred_element_type=F32)
    ys = jnp.dot(s_ref[...], ws_ref[...], preferred_element_type=F32)
    out_ref[...] = (ga_ref[...].astype(F32) * ya + gs_ref[...].astype(F32) * ys).astype(out_ref.dtype)


def _merge(o_att, o_ssm, w_att, w_ssm, proj, *, tm, tn, ga_col, gs_col):
    t, ka = o_att.shape
    ks = o_ssm.shape[1]
    n = w_att.shape[1]
    assert t % tm == 0 and n % tn == 0, (t, tm, n, tn)
    return pl.pallas_call(
        _merge_kernel,
        grid=(t // tm, n // tn),
        in_specs=[
            pl.BlockSpec((tm, ka), lambda i, j: (i, 0)),
            pl.BlockSpec((tm, ks), lambda i, j: (i, 0)),
            pl.BlockSpec((ka, tn), lambda i, j: (0, j)),
            pl.BlockSpec((ks, tn), lambda i, j: (0, j)),
            pl.BlockSpec((tm, tn), lambda i, j: (i, ga_col + j)),
            pl.BlockSpec((tm, tn), lambda i, j: (i, gs_col + j)),
        ],
        out_specs=pl.BlockSpec((tm, tn), lambda i, j: (i, j)),
        out_shape=jax.ShapeDtypeStruct((t, n), BF16),
        compiler_params=_params("parallel", "arbitrary"),
        name="merge",
    )(o_att, o_ssm, w_att, w_ssm, proj, proj)


def _outproj_kernel(m_ref, w_ref, h_ref, g_ref, out_ref):
    y = jnp.dot(m_ref[...], w_ref[...], preferred_element_type=F32)
    out_ref[...] = h_ref[...] + _rms_scale(y) * g_ref[...]


def _outproj(merged, w_out, h, post_g, *, tm):
    t, d = h.shape
    k = merged.shape[1]
    assert t % tm == 0, (t, tm)
    return pl.pallas_call(
        _outproj_kernel,
        grid=(t // tm,),
        in_specs=[
            pl.BlockSpec((tm, k), lambda i: (i, 0)),
            pl.BlockSpec((k, d), lambda i: (0, 0)),
            pl.BlockSpec((tm, d), lambda i: (i, 0)),
            pl.BlockSpec((1, d), lambda i: (0, 0)),
        ],
        out_specs=pl.BlockSpec((tm, d), lambda i: (i, 0)),
        out_shape=jax.ShapeDtypeStruct((t, d), F32),
        compiler_params=_params("parallel"),
        name="out_proj",
    )(merged, w_out, h, post_g)


def _pad_lanes(v):
    return jnp.pad(v.astype(F32), (0, V7X_LANES - v.shape[0]))[None, :]


def kernel(x, ffn1_pre_g, ffn1_w_gate, ffn1_w_up, ffn1_w_down, ffn1_post_g, mix_pre_g, w_in, att_lambda_q1, att_lambda_k1, att_lambda_q2, att_lambda_k2, att_subln_g, ssm_conv_w, ssm_conv_b, ssm_dt_bias, ssm_a_log, ssm_d, ssm_norm_g, w_branch_att, w_branch_ssm, w_out, mix_post_g, ffn2_pre_g, ffn2_w_gate, ffn2_w_up, ffn2_w_down, ffn2_post_g):
    bsz, seq, d = x.shape
    t = bsz * seq
    qk_w = ATT_HEADS * 2 * ATT_HEAD_DIM
    v_w = ATT_HEADS * ATT_V_DIM
    d_inner = ssm_norm_g.shape[1]
    n_heads = ssm_dt_bias.shape[1]
    bc_w = SSM_GROUPS * SSM_D_STATE
    conv_dim = d_inner + 2 * bc_w
    gw = d_inner // SSM_GROUPS

    w = w_in[0]
    o_q, o_k, o_v = 0, qk_w, 2 * qk_w
    o_z = o_v + v_w
    o_xbc = o_z + d_inner
    o_dt = o_xbc + conv_dim
    o_gate = o_dt + n_heads
    w_a = w[:, :o_dt].astype(BF16)
    w_g = w[:, o_gate:].astype(BF16)
    w_dt = jnp.pad(w[:, o_dt:o_gate], ((0, 0), (0, V7X_LANES - n_heads))).astype(BF16)
    c_q, c_k, c_v, c_z, c_x = o_q, o_k, o_v, o_z, o_xbc
    c_b = c_x + d_inner
    c_c = c_b + bc_w
    c_ga = c_x + conv_dim
    c_gs = c_ga + d

    h = x.reshape(t, d)
    h = _ffn(h, ffn1_pre_g, ffn1_w_gate[0].astype(BF16), ffn1_w_up[0].astype(BF16),
             ffn1_w_down[0].astype(BF16), ffn1_post_g, tm=512, tf=512)

    proj, dt_raw = _inproj(h, mix_pre_g, w_a, w_g, w_dt, ssm_conv_w[0], ssm_conv_b,
                           tm=1024, tn=1024, seq=seq, qk_w=qk_w, kv_end=o_z, z_end=o_xbc)
    proj3 = proj.reshape(bsz, seq, proj.shape[1])

    o_att = _attention(proj3, att_lambda_q1, att_lambda_k1, att_lambda_q2, att_lambda_k2,
                       att_subln_g, tq=512,
                       q_col=c_q // ATT_V_DIM, k_col=c_k // ATT_V_DIM, v_col=c_v // ATT_V_DIM)

    d_skip_row = jnp.repeat(ssm_d[0].astype(F32), SSM_HEAD_DIM)[None, :]
    o_ssm = _ssd(proj3, dt_raw.reshape(bsz, seq, V7X_LANES), _pad_lanes(ssm_dt_bias[0]),
                 _pad_lanes(ssm_a_log[0]), d_skip_row, ssm_norm_g,
                 n_grp=4, x_col=c_x, b_col=c_b, c_col=c_c, z_col=c_z)

    tn_merge = 512
    merged = _merge(o_att.reshape(t, v_w), o_ssm.reshape(t, d_inner),
                    w_branch_att[0].astype(BF16), w_branch_ssm[0].astype(BF16), proj,
                    tm=1024, tn=tn_merge, ga_col=c_ga // tn_merge, gs_col=c_gs // tn_merge)
    h = _outproj(merged, w_out[0].astype(BF16), h, mix_post_g, tm=512)

    h = _ffn(h, ffn2_pre_g, ffn2_w_gate[0].astype(BF16), ffn2_w_up[0].astype(BF16),
             ffn2_w_down[0].astype(BF16), ffn2_post_g, tm=512, tf=512)
    return h.reshape(bsz, seq, d)
```

```python
import functools
import math

import jax
import jax.numpy as jnp
from jax import lax
from jax.experimental import pallas as pl
from jax.experimental.pallas import tpu as pltpu

F32 = jnp.float32
BF16 = jnp.bfloat16

V7X_SUBLANES = 8
V7X_LANES = 128
V7X_VMEM_LIMIT_BYTES = 56 * 1024 * 1024

RMS_EPS = 1e-6

ATT_HEADS = 8
ATT_HEAD_DIM = 128
ATT_V_DIM = 2 * ATT_HEAD_DIM
SSM_HEAD_DIM = 64
SSM_GROUPS = 8
SSM_D_STATE = 128
SSM_CONV = 4
SSM_CHUNK = 128
LAMBDA_INIT = 0.8 - 0.6 * math.exp(-0.3 * 0)
LOG2_E = math.log2(math.e)

FFN_DOWN_CHUNK = 512
INPROJ_CHUNK = 256
INPROJ_CONV_ROWS = 128


def _params(*semantics):
    return pltpu.CompilerParams(dimension_semantics=semantics,
                                vmem_limit_bytes=V7X_VMEM_LIMIT_BYTES)


def _rms_scale(x):
    return x * lax.rsqrt(jnp.mean(x * x, axis=-1, keepdims=True) + RMS_EPS)


def _silu(x):
    return x / (1.0 + jnp.exp(-x))


def _sigmoid(x):
    return 1.0 / (1.0 + jnp.exp(-x))


def _ffn_kernel(h_ref, pre_g_ref, wg_ref, wu_ref, wd_ref, post_g_ref, out_ref, xn_ref, acc_ref):
    j = pl.program_id(1)

    @pl.when(j == 0)
    def _():
        xn_ref[...] = (_rms_scale(h_ref[...]) * pre_g_ref[...]).astype(BF16)
        acc_ref[...] = jnp.zeros(acc_ref.shape, F32)

    xn = xn_ref[...]
    gate = jnp.dot(xn, wg_ref[...], preferred_element_type=F32)
    up = jnp.dot(xn, wu_ref[...], preferred_element_type=F32)
    act = (_silu(gate) * up).astype(BF16)
    d = acc_ref.shape[1]
    for c in range(d // FFN_DOWN_CHUNK):
        cols = slice(c * FFN_DOWN_CHUNK, (c + 1) * FFN_DOWN_CHUNK)
        acc_ref[:, cols] += jnp.dot(act, wd_ref[:, cols], preferred_element_type=F32)

    @pl.when(j == pl.num_programs(1) - 1)
    def _():
        out_ref[...] = h_ref[...] + 0.5 * (_rms_scale(acc_ref[...]) * post_g_ref[...])


def _ffn(h, pre_g, w_gate, w_up, w_down, post_g, *, tm, tf):
    t, d = h.shape
    f = w_gate.shape[1]
    assert t % tm == 0 and f % tf == 0, (t, tm, f, tf)
    return pl.pallas_call(
        _ffn_kernel,
        grid=(t // tm, f // tf),
        in_specs=[
            pl.BlockSpec((tm, d), lambda i, j: (i, 0)),
            pl.BlockSpec((1, d), lambda i, j: (0, 0)),
            pl.BlockSpec((d, tf), lambda i, j: (0, j)),
            pl.BlockSpec((d, tf), lambda i, j: (0, j)),
            pl.BlockSpec((tf, d), lambda i, j: (j, 0)),
            pl.BlockSpec((1, d), lambda i, j: (0, 0)),
        ],
        out_specs=pl.BlockSpec((tm, d), lambda i, j: (i, 0)),
        out_shape=jax.ShapeDtypeStruct((t, d), F32),
        scratch_shapes=[pltpu.VMEM((tm, d), BF16), pltpu.VMEM((tm, d), F32)],
        compiler_params=_params("parallel", "arbitrary"),
        name="ffn",
    )(h, pre_g, w_gate, w_up, w_down, post_g)


def _inproj_kernel(h_ref, g_ref, wa_ref, wg_ref, wdt_ref, cw_ref, cb_ref, out_ref, dt_ref,
                   xn_ref, ext_ref, tail_ref, *, j_k, j_z, j_x, j_g, tiles_per_seq):
    i = pl.program_id(0)
    j = pl.program_id(1)
    tm, tn = out_ref.shape
    tail = V7X_SUBLANES

    @pl.when(j == 0)
    def _():
        xn = (_rms_scale(h_ref[...]) * g_ref[...]).astype(BF16)
        xn_ref[...] = xn
        dt_ref[...] = jnp.dot(xn, wdt_ref[...], preferred_element_type=F32)

        @pl.when(i == 0)
        def _():
            tail_ref[...] = jnp.zeros(tail_ref.shape, F32)

    def project(w_ref, epilogue):
        xn = xn_ref[...]
        for c in range(tn // INPROJ_CHUNK):
            cols = slice(c * INPROJ_CHUNK, (c + 1) * INPROJ_CHUNK)
            epilogue(jnp.dot(xn, w_ref[:, cols], preferred_element_type=F32), cols)

    def elementwise(fn):
        def epilogue(y, cols):
            out_ref[:, cols] = fn(y).astype(out_ref.dtype)
        return epilogue

    def conv_silu(y, cols):
        jx = j - j_x
        ext_ref[tail:tail + tm, cols] = y
        ext_ref[0:tail, cols] = tail_ref[jx, :, cols]
        sub = lax.broadcasted_iota(jnp.int32, (INPROJ_CONV_ROWS, INPROJ_CHUNK), 0) % tail
        for r in range(tm // INPROJ_CONV_ROWS):
            r0 = r * INPROJ_CONV_ROWS
            prev = ext_ref[r0:r0 + INPROJ_CONV_ROWS, cols]
            cur = ext_ref[r0 + tail:r0 + tail + INPROJ_CONV_ROWS, cols]
            acc = cb_ref[:, cols] + cw_ref[SSM_CONV - 1:SSM_CONV, cols] * cur
            for s in range(1, SSM_CONV):
                k = SSM_CONV - 1 - s
                merged = jnp.where(sub >= tail - s, prev, cur).reshape(-1, tail, INPROJ_CHUNK)
                shifted = pltpu.roll(merged, s, 1).reshape(INPROJ_CONV_ROWS, INPROJ_CHUNK)
                acc = acc + cw_ref[k:k + 1, cols] * shifted
            out_ref[r0:r0 + INPROJ_CONV_ROWS, cols] = _silu(acc).astype(out_ref.dtype)
        seq_ends = (i % tiles_per_seq) == tiles_per_seq - 1
        tail_ref[jx, :, cols] = jnp.where(seq_ends, 0.0, ext_ref[tm:tm + tail, cols])

    @pl.when(j < j_k)
    def _():
        project(wa_ref, elementwise(lambda y: y * (LOG2_E * ATT_HEAD_DIM ** -0.5)))

    @pl.when(jnp.logical_and(j >= j_k, j < j_z))
    def _():
        project(wa_ref, elementwise(lambda y: y))

    @pl.when(jnp.logical_and(j >= j_z, j < j_x))
    def _():
        project(wa_ref, elementwise(_silu))

    @pl.when(jnp.logical_and(j >= j_x, j < j_g))
    def _():
        project(wa_ref, conv_silu)

    @pl.when(j >= j_g)
    def _():
        project(wg_ref, elementwise(_sigmoid))


def _inproj(h, g, w_all, w_g, w_dt, conv_w, conv_b, *, tm, tn, seq, qk_w, kv_end, z_end, a_cols):
    t, d = h.shape
    na, ng = a_cols, w_g.shape[1]
    ndt = w_dt.shape[1]
    conv_dim = conv_w.shape[1]
    assert t % tm == 0 and seq % tm == 0, (t, seq, tm)
    assert all(v % tn == 0 for v in (qk_w, kv_end, z_end, na, ng, conv_dim)), (tn,)
    assert z_end + conv_dim == na and tn % INPROJ_CHUNK == 0 and na <= w_all.shape[1]
    j_k, j_z, j_x, j_g = qk_w // tn, kv_end // tn, z_end // tn, na // tn
    n_conv = conv_dim // tn
    return pl.pallas_call(
        functools.partial(_inproj_kernel, j_k=j_k, j_z=j_z, j_x=j_x, j_g=j_g,
                          tiles_per_seq=seq // tm),
        grid=(t // tm, (na + ng) // tn),
        in_specs=[
            pl.BlockSpec((tm, d), lambda i, j: (i, 0)),
            pl.BlockSpec((1, d), lambda i, j: (0, 0)),
            pl.BlockSpec((d, tn), lambda i, j: (0, jnp.minimum(j, j_g - 1))),
            pl.BlockSpec((d, tn), lambda i, j: (0, jnp.maximum(j - j_g, 0))),
            pl.BlockSpec((d, ndt), lambda i, j: (0, 0)),
            pl.BlockSpec((SSM_CONV, tn), lambda i, j: (0, jnp.clip(j - j_x, 0, n_conv - 1))),
            pl.BlockSpec((1, tn), lambda i, j: (0, jnp.clip(j - j_x, 0, n_conv - 1))),
        ],
        out_specs=[
            pl.BlockSpec((tm, tn), lambda i, j: (i, j)),
            pl.BlockSpec((tm, ndt), lambda i, j: (i, 0)),
        ],
        out_shape=[jax.ShapeDtypeStruct((t, na + ng), BF16), jax.ShapeDtypeStruct((t, ndt), F32)],
        scratch_shapes=[
            pltpu.VMEM((tm, d), BF16),
            pltpu.VMEM((tm + V7X_SUBLANES, tn), F32),
            pltpu.VMEM((n_conv, V7X_SUBLANES, tn), F32),
        ],
        compiler_params=_params("arbitrary", "arbitrary"),
        name="in_proj",
    )(h, g, w_all, w_g, w_dt, conv_w, conv_b)


def _attn_kernel(lq1_ref, lk1_ref, lq2_ref, lk2_ref, g_ref, q_ref, k_ref, v_ref, o_ref,
                 vt_ref, m_ref, l_ref, acc_ref, sa_ref, sb_ref, *, tq):
    qi = pl.program_id(2)
    dh = ATT_HEAD_DIM
    n_kv = vt_ref.shape[0]

    @pl.when(qi == 0)
    def _():
        for j in range(n_kv):
            vt_ref[j] = v_ref[0, j * tq:(j + 1) * tq, :].T

    m_ref[...] = jnp.full(m_ref.shape, -jnp.inf, F32)
    l_ref[...] = jnp.zeros(l_ref.shape, F32)
    acc_ref[...] = jnp.zeros(acc_ref.shape, F32)

    def scores(j, s_ref):
        start = pl.multiple_of(j * tq, tq)
        for mi in range(2):
            q = q_ref[0, :, mi * dh:(mi + 1) * dh]
            k = k_ref[0, pl.ds(start, tq), mi * dh:(mi + 1) * dh]
            s_ref[mi] = lax.dot_general(k, q, (((1,), (1,)), ((), ())), preferred_element_type=F32)

    def softmax_pv(j, s_ref, masked):
        vt = vt_ref[j]
        for mi in range(2):
            st = s_ref[mi]
            if masked:
                kpos = lax.broadcasted_iota(jnp.int32, st.shape, 0)
                qpos = lax.broadcasted_iota(jnp.int32, st.shape, 1)
                st = jnp.where(kpos <= qpos, st, -jnp.inf)
            m_prev = m_ref[mi]
            m_new = jnp.maximum(m_prev, jnp.max(st, axis=0, keepdims=True))
            alpha = jnp.exp2(m_prev - m_new)
            p = jnp.exp2(st - m_new)
            l_ref[mi] = alpha * l_ref[mi] + jnp.sum(p, axis=0, keepdims=True)
            acc_ref[mi] = alpha * acc_ref[mi] + jnp.dot(vt, p.astype(BF16), preferred_element_type=F32)
            m_ref[mi] = m_new

    scores(0, sa_ref)

    def tile_pair(i, carry):
        j = 2 * i
        scores(j + 1, sb_ref)
        softmax_pv(j, sa_ref, masked=False)
        scores(j + 2, sa_ref)
        softmax_pv(j + 1, sb_ref, masked=False)
        return carry

    lax.fori_loop(0, qi // 2, tile_pair, 0)

    @pl.when(qi % 2 == 0)
    def _():
        softmax_pv(qi, sa_ref, masked=True)

    @pl.when(qi % 2 == 1)
    def _():
        scores(qi, sb_ref)
        softmax_pv(qi - 1, sa_ref, masked=False)
        softmax_pv(qi, sb_ref, masked=True)

    lam = (jnp.exp(jnp.sum(lq1_ref[...] * lk1_ref[...], axis=-1, keepdims=True))
           - jnp.exp(jnp.sum(lq2_ref[...] * lk2_ref[...], axis=-1, keepdims=True))
           + LAMBDA_INIT)
    ot = acc_ref[0] / l_ref[0] - lam * (acc_ref[1] / l_ref[1])
    ot = ot * lax.rsqrt(jnp.mean(ot * ot, axis=0, keepdims=True) + RMS_EPS)
    o_ref[0] = (ot.T * g_ref[...] * (1.0 - LAMBDA_INIT)).astype(o_ref.dtype)


def _attention(proj, lq1, lk1, lq2, lk2, subln_g, *, tq, q_col, k_col, v_col):
    b, s, _ = proj.shape
    w = ATT_V_DIM
    assert s % tq == 0, (s, tq)
    vec = pl.BlockSpec((1, ATT_HEAD_DIM), lambda bi, h, qi: (0, 0))
    return pl.pallas_call(
        functools.partial(_attn_kernel, tq=tq),
        grid=(b, ATT_HEADS, s // tq),
        in_specs=[
            vec, vec, vec, vec,
            pl.BlockSpec((1, w), lambda bi, h, qi: (0, 0)),
            pl.BlockSpec((1, tq, w), lambda bi, h, qi: (bi, qi, q_col + h)),
            pl.BlockSpec((1, s, w), lambda bi, h, qi: (bi, 0, k_col + h)),
            pl.BlockSpec((1, s, w), lambda bi, h, qi: (bi, 0, v_col + h)),
        ],
        out_specs=pl.BlockSpec((1, tq, w), lambda bi, h, qi: (bi, qi, h)),
        out_shape=jax.ShapeDtypeStruct((b, s, ATT_HEADS * w), BF16),
        scratch_shapes=[
            pltpu.VMEM((s // tq, w, tq), BF16),
            pltpu.VMEM((2, 1, tq), F32),
            pltpu.VMEM((2, 1, tq), F32),
            pltpu.VMEM((2, w, tq), F32),
            pltpu.VMEM((2, tq, tq), F32),
            pltpu.VMEM((2, tq, tq), F32),
        ],
        compiler_params=_params("parallel", "parallel", "arbitrary"),
        name="diff_attn",
    )(lq1, lk1, lq2, lk2, subln_g, proj, proj, proj)


def _split3(x):
    p1 = x.astype(BF16)
    r1 = x - p1.astype(F32)
    p2 = r1.astype(BF16)
    p3 = (r1 - p2.astype(F32)).astype(BF16)
    return p1, p2, p3


def _ssd_kernel(dt_ref, dtb_ref, alog_ref, x_ref, b_ref, c_ref, z_ref, dskip_ref, ng_ref,
                o_ref,
                state_ref, dt_all_ref, acs_ref, acst_ref):
    c = pl.program_id(1)
    gblk = pl.program_id(2)
    ln = SSM_CHUNK
    hp = SSM_HEAD_DIM
    ns = SSM_D_STATE
    n_grp = b_ref.shape[-1] // ns
    gw = x_ref.shape[-1] // n_grp
    heads = gw // hp

    row = lax.broadcasted_iota(jnp.int32, (ln, ln), 0)
    col = lax.broadcasted_iota(jnp.int32, (ln, ln), 1)
    tril = col <= row

    @pl.when(gblk == 0)
    def _():
        xb = dt_ref[0] + dtb_ref[...]
        dt_all = jnp.maximum(xb, 0.0) + jnp.log1p(jnp.exp(-jnp.abs(xb)))
        adt = dt_all * (-jnp.exp(alog_ref[...]))
        ltri = jnp.where(tril, 1.0, 0.0).astype(BF16)
        acs_all = sum(jnp.dot(ltri, part, preferred_element_type=F32) for part in _split3(adt))
        dt_all_ref[...] = dt_all
        acs_ref[...] = acs_all
        acst_ref[...] = acs_all.T

    shift = lax.rem(V7X_LANES - n_grp * heads * gblk, V7X_LANES)
    dt_blk = pltpu.roll(dt_all_ref[...], shift, 1)
    acs_blk = pltpu.roll(acs_ref[...], shift, 1)

    lane = lax.broadcasted_iota(jnp.int32, (ln, 2 * hp), 1)
    lo = lane < hp
    hi = jnp.logical_not(lo)

    for gi in range(n_grp):
        g = gblk * n_grp + gi
        gcols = slice(gi * gw, (gi + 1) * gw)
        scols = slice(gi * ns, (gi + 1) * ns)

        @pl.when(c == 0)
        def _():
            state_ref[g] = jnp.zeros(state_ref.shape[1:], F32)

        xc = x_ref[0, :, gcols].astype(F32)
        bc = b_ref[0, :, scols]
        cc = c_ref[0, :, scols]
        cb = lax.dot_general(cc, bc, (((1,), (1,)), ((), ())), preferred_element_type=F32)
        state = state_ref[g]
        y_off = jnp.dot(cc, state.astype(BF16), preferred_element_type=F32)

        y_parts = []
        xw_parts = []
        dec_last_parts = []
        for pr in range(heads // 2):
            sl = slice(pr * 2 * hp, (pr + 1) * 2 * hp)
            h0 = gi * heads + 2 * pr
            acs_b = [jnp.broadcast_to(acs_blk[:, h:h + 1], (ln, 2 * hp)) for h in (h0, h0 + 1)]
            dt_pair = jnp.where(lo, dt_blk[:, h0:h0 + 1], dt_blk[:, h0 + 1:h0 + 2])
            acs_pair = jnp.where(lo, acs_b[0], acs_b[1])
            xdt = xc[:, sl] * dt_pair
            y_pair = y_off[:, sl] * jnp.exp(acs_pair)
            for k, keep in enumerate((lo, hi)):
                rowv = acst_ref[pl.ds(g * heads + 2 * pr + k, 1), :]
                decay = jnp.exp(jnp.where(tril, acs_b[k] - rowv, -jnp.inf))
                m = (cb * decay).astype(BF16)
                y_pair = y_pair + jnp.dot(m, jnp.where(keep, xdt, 0.0).astype(BF16),
                                          preferred_element_type=F32)
            y_parts.append(y_pair)
            last = acs_pair[ln - 1:ln, :]
            xw_parts.append((xdt * jnp.exp(last - acs_pair)).astype(BF16))
            dec_last_parts.append(jnp.exp(last))

        xw = jnp.concatenate(xw_parts, axis=1)
        dec_last = jnp.concatenate(dec_last_parts, axis=1)
        bct = bc.astype(F32).T.astype(BF16)
        state_ref[g] = state * dec_last + jnp.dot(bct, xw, preferred_element_type=F32)

        y = jnp.concatenate(y_parts, axis=1)
        y = (y + xc * dskip_ref[:, gcols]) * z_ref[0, :, gcols].astype(F32)
        o_ref[0, :, gcols] = (_rms_scale(y) * ng_ref[:, gcols]).astype(o_ref.dtype)


def _ssd(proj, dt_raw, dt_bias, a_log, d_skip_row, norm_g, *, n_grp, x_col, b_col, c_col, z_col):
    bsz, s, _ = proj.shape
    ln = SSM_CHUNK
    gw = d_skip_row.shape[1] // SSM_GROUPS
    xw, sw = n_grp * gw, n_grp * SSM_D_STATE
    assert s % ln == 0 and gw % (2 * SSM_HEAD_DIM) == 0 and SSM_GROUPS % n_grp == 0, (s, ln, gw)
    assert x_col % xw == 0 and z_col % xw == 0 and b_col % sw == 0 and c_col % sw == 0
    row128 = pl.BlockSpec((1, V7X_LANES), lambda b, c, g: (0, 0))
    return pl.pallas_call(
        _ssd_kernel,
        grid=(bsz, s // ln, SSM_GROUPS // n_grp),
        in_specs=[
            pl.BlockSpec((1, ln, V7X_LANES), lambda b, c, g: (b, c, 0)),
            row128, row128,
            pl.BlockSpec((1, ln, xw), lambda b, c, g: (b, c, x_col // xw + g)),
            pl.BlockSpec((1, ln, sw), lambda b, c, g: (b, c, b_col // sw + g)),
            pl.BlockSpec((1, ln, sw), lambda b, c, g: (b, c, c_col // sw + g)),
            pl.BlockSpec((1, ln, xw), lambda b, c, g: (b, c, z_col // xw + g)),
            pl.BlockSpec((1, xw), lambda b, c, g: (0, g)),
            pl.BlockSpec((1, xw), lambda b, c, g: (0, g)),
        ],
        out_specs=pl.BlockSpec((1, ln, xw), lambda b, c, g: (b, c, g)),
        out_shape=jax.ShapeDtypeStruct((bsz, s, SSM_GROUPS * gw), BF16),
        scratch_shapes=[
            pltpu.VMEM((SSM_GROUPS, SSM_D_STATE, gw), F32),
            pltpu.VMEM((ln, V7X_LANES), F32),
            pltpu.VMEM((ln, V7X_LANES), F32),
            pltpu.VMEM((V7X_LANES, ln), F32),
        ],
        compiler_params=_params("parallel", "arbitrary", "arbitrary"),
        name="ssd",
    )(dt_raw, dt_bias, a_log, proj, proj, proj, proj, d_skip_row, norm_g)


def _merge_kernel(a_ref, s_ref, wa_ref, ws_ref, ga_ref, gs_ref, out_ref):
    ya = jnp.dot(a_ref[...], wa_ref[...], preferred_element_type=F32)
    ys = jnp.dot(s_ref[...], ws_ref[...], preferred_element_type=F32)
    out_ref[...] = (ga_ref[...].astype(F32) * ya + gs_ref[...].astype(F32) * ys).astype(out_ref.dtype)


def _merge(o_att, o_ssm, w_att, w_ssm, proj, *, tm, tn, ga_col, gs_col):
    t, ka = o_att.shape
    ks = o_ssm.shape[1]
    n = w_att.shape[1]
    assert t % tm == 0 and n % tn == 0, (t, tm, n, tn)
    return pl.pallas_call(
        _merge_kernel,
        grid=(t // tm, n // tn),
        in_specs=[
            pl.BlockSpec((tm, ka), lambda i, j: (i, 0)),
            pl.BlockSpec((tm, ks), lambda i, j: (i, 0)),
            pl.BlockSpec((ka, tn), lambda i, j: (0, j)),
            pl.BlockSpec((ks, tn), lambda i, j: (0, j)),
            pl.BlockSpec((tm, tn), lambda i, j: (i, ga_col + j)),
            pl.BlockSpec((tm, tn), lambda i, j: (i, gs_col + j)),
        ],
        out_specs=pl.BlockSpec((tm, tn), lambda i, j: (i, j)),
        out_shape=jax.ShapeDtypeStruct((t, n), BF16),
        compiler_params=_params("parallel", "arbitrary"),
        name="merge",
    )(o_att, o_ssm, w_att, w_ssm, proj, proj)


def _outproj_kernel(m_ref, w_ref, h_ref, g_ref, out_ref):
    y = jnp.dot(m_ref[...], w_ref[...], preferred_element_type=F32)
    out_ref[...] = h_ref[...] + _rms_scale(y) * g_ref[...]


def _outproj(merged, w_out, h, post_g, *, tm):
    t, d = h.shape
    k = merged.shape[1]
    assert t % tm == 0, (t, tm)
    return pl.pallas_call(
        _outproj_kernel,
        grid=(t // tm,),
        in_specs=[
            pl.BlockSpec((tm, k), lambda i: (i, 0)),
            pl.BlockSpec((k, d), lambda i: (0, 0)),
            pl.BlockSpec((tm, d), lambda i: (i, 0)),
            pl.BlockSpec((1, d), lambda i: (0, 0)),
        ],
        out_specs=pl.BlockSpec((tm, d), lambda i: (i, 0)),
        out_shape=jax.ShapeDtypeStruct((t, d), F32),
        compiler_params=_params("parallel"),
        name="out_proj",
    )(merged, w_out, h, post_g)


def _pad_lanes(v):
    return jnp.pad(v.astype(F32), (0, V7X_LANES - v.shape[0]))[None, :]


def kernel(x, ffn1_pre_g, ffn1_w_gate, ffn1_w_up, ffn1_w_down, ffn1_post_g, mix_pre_g, w_in, att_lambda_q1, att_lambda_k1, att_lambda_q2, att_lambda_k2, att_subln_g, ssm_conv_w, ssm_conv_b, ssm_dt_bias, ssm_a_log, ssm_d, ssm_norm_g, w_branch_att, w_branch_ssm, w_out, mix_post_g, ffn2_pre_g, ffn2_w_gate, ffn2_w_up, ffn2_w_down, ffn2_post_g):
    bsz, seq, d = x.shape
    t = bsz * seq
    qk_w = ATT_HEADS * 2 * ATT_HEAD_DIM
    v_w = ATT_HEADS * ATT_V_DIM
    d_inner = ssm_norm_g.shape[1]
    n_heads = ssm_dt_bias.shape[1]
    bc_w = SSM_GROUPS * SSM_D_STATE
    conv_dim = d_inner + 2 * bc_w

    w = w_in[0].astype(BF16)
    o_q, o_k, o_v = 0, qk_w, 2 * qk_w
    o_z = o_v + v_w
    o_xbc = o_z + d_inner
    o_dt = o_xbc + conv_dim
    o_gate = o_dt + n_heads
    w_g = w[:, o_gate:]
    w_dt = jnp.pad(w[:, o_dt:o_gate], ((0, 0), (0, V7X_LANES - n_heads)))
    c_q, c_k, c_v, c_z, c_x = o_q, o_k, o_v, o_z, o_xbc
    c_b = c_x + d_inner
    c_c = c_b + bc_w
    c_ga = c_x + conv_dim
    c_gs = c_ga + d

    h = x.reshape(t, d)
    h = _ffn(h, ffn1_pre_g, ffn1_w_gate[0].astype(BF16), ffn1_w_up[0].astype(BF16),
             ffn1_w_down[0].astype(BF16), ffn1_post_g, tm=512, tf=512)

    proj, dt_raw = _inproj(h, mix_pre_g, w, w_g, w_dt, ssm_conv_w[0], ssm_conv_b,
                           tm=1024, tn=1024, seq=seq, qk_w=qk_w, kv_end=o_z, z_end=o_xbc,
                           a_cols=o_dt)
    proj3 = proj.reshape(bsz, seq, proj.shape[1])

    o_att = _attention(proj3, att_lambda_q1, att_lambda_k1, att_lambda_q2, att_lambda_k2,
                       att_subln_g, tq=512,
                       q_col=c_q // ATT_V_DIM, k_col=c_k // ATT_V_DIM, v_col=c_v // ATT_V_DIM)

    d_skip_row = jnp.repeat(ssm_d[0].astype(F32), SSM_HEAD_DIM)[None, :]
    o_ssm = _ssd(proj3, dt_raw.reshape(bsz, seq, V7X_LANES), _pad_lanes(ssm_dt_bias[0]),
                 _pad_lanes(ssm_a_log[0]), d_skip_row, ssm_norm_g,
                 n_grp=4, x_col=c_x, b_col=c_b, c_col=c_c, z_col=c_z)

    tn_merge = 512
    merged = _merge(o_att.reshape(t, v_w), o_ssm.reshape(t, d_inner),
                    w_branch_att[0].astype(BF16), w_branch_ssm[0].astype(BF16), proj,
                    tm=1024, tn=tn_merge, ga_col=c_ga // tn_merge, gs_col=c_gs // tn_merge)
    h = _outproj(merged, w_out[0].astype(BF16), h, mix_post_g, tm=512)

    h = _ffn(h, ffn2_pre_g, ffn2_w_gate[0].astype(BF16), ffn2_w_up[0].astype(BF16),
             ffn2_w_down[0].astype(BF16), ffn2_post_g, tm=512, tf=512)
    return h.reshape(bsz, seq, d)
```

```python
import functools
import math

import jax
import jax.numpy as jnp
from jax import lax
from jax.experimental import pallas as pl
from jax.experimental.pallas import tpu as pltpu

F32 = jnp.float32
BF16 = jnp.bfloat16

V7X_SUBLANES = 8
V7X_LANES = 128
V7X_VMEM_LIMIT_BYTES = 56 * 1024 * 1024

RMS_EPS = 1e-6

ATT_HEADS = 8
ATT_HEAD_DIM = 128
ATT_V_DIM = 2 * ATT_HEAD_DIM
SSM_HEAD_DIM = 64
SSM_GROUPS = 8
SSM_D_STATE = 128
SSM_CONV = 4
SSM_CHUNK = 128
LAMBDA_INIT = 0.8 - 0.6 * math.exp(-0.3 * 0)
LOG2_E = math.log2(math.e)

FFN_DOWN_CHUNK = 512
INPROJ_CHUNK = 256
INPROJ_CONV_ROWS = 128


def _params(*semantics):
    return pltpu.CompilerParams(dimension_semantics=semantics,
                                vmem_limit_bytes=V7X_VMEM_LIMIT_BYTES)


def _rms_scale(x):
    return x * lax.rsqrt(jnp.mean(x * x, axis=-1, keepdims=True) + RMS_EPS)


def _silu(x):
    return x / (1.0 + jnp.exp(-x))


def _sigmoid(x):
    return 1.0 / (1.0 + jnp.exp(-x))


def _ffn_kernel(h_ref, pre_g_ref, wg_ref, wu_ref, wd_ref, post_g_ref, out_ref, xn_ref, acc_ref):
    j = pl.program_id(1)

    @pl.when(j == 0)
    def _():
        xn_ref[...] = (_rms_scale(h_ref[...]) * pre_g_ref[...]).astype(BF16)
        acc_ref[...] = jnp.zeros(acc_ref.shape, F32)

    xn = xn_ref[...]
    gate = jnp.dot(xn, wg_ref[...], preferred_element_type=F32)
    up = jnp.dot(xn, wu_ref[...], preferred_element_type=F32)
    act = (_silu(gate) * up).astype(BF16)
    d = acc_ref.shape[1]
    for c in range(d // FFN_DOWN_CHUNK):
        cols = slice(c * FFN_DOWN_CHUNK, (c + 1) * FFN_DOWN_CHUNK)
        acc_ref[:, cols] += jnp.dot(act, wd_ref[:, cols], preferred_element_type=F32)

    @pl.when(j == pl.num_programs(1) - 1)
    def _():
        out_ref[...] = h_ref[...] + 0.5 * (_rms_scale(acc_ref[...]) * post_g_ref[...])


def _ffn(h, pre_g, w_gate, w_up, w_down, post_g, *, tm, tf):
    t, d = h.shape
    f = w_gate.shape[1]
    assert t % tm == 0 and f % tf == 0, (t, tm, f, tf)
    return pl.pallas_call(
        _ffn_kernel,
        grid=(t // tm, f // tf),
        in_specs=[
            pl.BlockSpec((tm, d), lambda i, j: (i, 0)),
            pl.BlockSpec((1, d), lambda i, j: (0, 0)),
            pl.BlockSpec((d, tf), lambda i, j: (0, j)),
            pl.BlockSpec((d, tf), lambda i, j: (0, j)),
            pl.BlockSpec((tf, d), lambda i, j: (j, 0)),
            pl.BlockSpec((1, d), lambda i, j: (0, 0)),
        ],
        out_specs=pl.BlockSpec((tm, d), lambda i, j: (i, 0)),
        out_shape=jax.ShapeDtypeStruct((t, d), F32),
        scratch_shapes=[pltpu.VMEM((tm, d), BF16), pltpu.VMEM((tm, d), F32)],
        compiler_params=_params("parallel", "arbitrary"),
        name="ffn",
    )(h, pre_g, w_gate, w_up, w_down, post_g)


def _inproj_kernel(h_ref, g_ref, wa_ref, wg_ref, wdt_ref, cw_ref, cb_ref, out_ref, dt_ref,
                   xn_ref, tail_ref, *, j_k, j_z, j_x, j_g, tiles_per_seq):
    i = pl.program_id(0)
    j = pl.program_id(1)
    tm, tn = out_ref.shape
    tail = V7X_SUBLANES

    @pl.when(j == 0)
    def _():
        xn = (_rms_scale(h_ref[...]) * g_ref[...]).astype(BF16)
        xn_ref[...] = xn
        dt_ref[...] = jnp.dot(xn, wdt_ref[...], preferred_element_type=F32)

        @pl.when(i == 0)
        def _():
            tail_ref[...] = jnp.zeros(tail_ref.shape, F32)

    def project(w_ref, epilogue):
        xn = xn_ref[...]
        for c in range(tn // INPROJ_CHUNK):
            cols = slice(c * INPROJ_CHUNK, (c + 1) * INPROJ_CHUNK)
            epilogue(jnp.dot(xn, w_ref[:, cols], preferred_element_type=F32), cols)

    def elementwise(fn):
        def epilogue(y, cols):
            out_ref[:, cols] = fn(y).astype(out_ref.dtype)
        return epilogue

    def conv_silu(y, cols):
        jx = j - j_x
        rb = INPROJ_CONV_ROWS
        sub = lax.broadcasted_iota(jnp.int32, (rb, INPROJ_CHUNK), 0) % tail
        for r in range(tm // rb):
            r0 = r * rb
            cur = y[r0:r0 + rb]
            if r == 0:
                prev = jnp.concatenate([tail_ref[jx, :, cols], y[0:rb - tail]], axis=0)
            else:
                prev = y[r0 - tail:r0 - tail + rb]
            acc = cb_ref[:, cols] + cw_ref[SSM_CONV - 1:SSM_CONV, cols] * cur
            for s in range(1, SSM_CONV):
                k = SSM_CONV - 1 - s
                merged = jnp.where(sub >= tail - s, prev, cur).reshape(-1, tail, INPROJ_CHUNK)
                shifted = pltpu.roll(merged, s, 1).reshape(rb, INPROJ_CHUNK)
                acc = acc + cw_ref[k:k + 1, cols] * shifted
            out_ref[r0:r0 + rb, cols] = _silu(acc).astype(out_ref.dtype)
        seq_ends = (i % tiles_per_seq) == tiles_per_seq - 1
        tail_ref[jx, :, cols] = jnp.where(seq_ends, 0.0, y[tm - tail:tm])

    @pl.when(j < j_k)
    def _():
        project(wa_ref, elementwise(lambda y: y * (LOG2_E * ATT_HEAD_DIM ** -0.5)))

    @pl.when(jnp.logical_and(j >= j_k, j < j_z))
    def _():
        project(wa_ref, elementwise(lambda y: y))

    @pl.when(jnp.logical_and(j >= j_z, j < j_x))
    def _():
        project(wa_ref, elementwise(_silu))

    @pl.when(jnp.logical_and(j >= j_x, j < j_g))
    def _():
        project(wa_ref, conv_silu)

    @pl.when(j >= j_g)
    def _():
        project(wg_ref, elementwise(_sigmoid))


def _inproj(h, g, w_all, w_g, w_dt, conv_w, conv_b, *, tm, tn, seq, qk_w, kv_end, z_end, a_cols):
    t, d = h.shape
    na, ng = a_cols, w_g.shape[1]
    ndt = w_dt.shape[1]
    conv_dim = conv_w.shape[1]
    assert t % tm == 0 and seq % tm == 0, (t, seq, tm)
    assert all(v % tn == 0 for v in (qk_w, kv_end, z_end, na, ng, conv_dim)), (tn,)
    assert z_end + conv_dim == na and tn % INPROJ_CHUNK == 0 and na <= w_all.shape[1]
    j_k, j_z, j_x, j_g = qk_w // tn, kv_end // tn, z_end // tn, na // tn
    n_conv = conv_dim // tn
    return pl.pallas_call(
        functools.partial(_inproj_kernel, j_k=j_k, j_z=j_z, j_x=j_x, j_g=j_g,
                          tiles_per_seq=seq // tm),
        grid=(t // tm, (na + ng) // tn),
        in_specs=[
            pl.BlockSpec((tm, d), lambda i, j: (i, 0)),
            pl.BlockSpec((1, d), lambda i, j: (0, 0)),
            pl.BlockSpec((d, tn), lambda i, j: (0, jnp.minimum(j, j_g - 1))),
            pl.BlockSpec((d, tn), lambda i, j: (0, jnp.maximum(j - j_g, 0))),
            pl.BlockSpec((d, ndt), lambda i, j: (0, 0)),
            pl.BlockSpec((SSM_CONV, tn), lambda i, j: (0, jnp.clip(j - j_x, 0, n_conv - 1))),
            pl.BlockSpec((1, tn), lambda i, j: (0, jnp.clip(j - j_x, 0, n_conv - 1))),
        ],
        out_specs=[
            pl.BlockSpec((tm, tn), lambda i, j: (i, j)),
            pl.BlockSpec((tm, ndt), lambda i, j: (i, 0)),
        ],
        out_shape=[jax.ShapeDtypeStruct((t, na + ng), BF16), jax.ShapeDtypeStruct((t, ndt), F32)],
        scratch_shapes=[
            pltpu.VMEM((tm, d), BF16),
            pltpu.VMEM((n_conv, V7X_SUBLANES, tn), F32),
        ],
        compiler_params=_params("arbitrary", "arbitrary"),
        name="in_proj",
    )(h, g, w_all, w_g, w_dt, conv_w, conv_b)


def _attn_kernel(lq1_ref, lk1_ref, lq2_ref, lk2_ref, g_ref, q_ref, k_ref, v_ref, o_ref,
                 vt_ref, m_ref, l_ref, acc_ref, sa_ref, sb_ref, *, tq):
    qi = pl.program_id(2)
    dh = ATT_HEAD_DIM
    n_kv = vt_ref.shape[0]

    @pl.when(qi == 0)
    def _():
        for j in range(n_kv):
            vt_ref[j] = v_ref[0, j * tq:(j + 1) * tq, :].T

    m_ref[...] = jnp.full(m_ref.shape, -jnp.inf, F32)
    l_ref[...] = jnp.zeros(l_ref.shape, F32)
    acc_ref[...] = jnp.zeros(acc_ref.shape, F32)

    def scores(j, s_ref):
        start = pl.multiple_of(j * tq, tq)
        for mi in range(2):
            q = q_ref[0, :, mi * dh:(mi + 1) * dh]
            k = k_ref[0, pl.ds(start, tq), mi * dh:(mi + 1) * dh]
            s_ref[mi] = lax.dot_general(k, q, (((1,), (1,)), ((), ())), preferred_element_type=F32)

    def softmax_pv(j, s_ref, masked):
        vt = vt_ref[j]
        for mi in range(2):
            st = s_ref[mi]
            if masked:
                kpos = lax.broadcasted_iota(jnp.int32, st.shape, 0)
                qpos = lax.broadcasted_iota(jnp.int32, st.shape, 1)
                st = jnp.where(kpos <= qpos, st, -jnp.inf)
            m_prev = m_ref[mi]
            m_new = jnp.maximum(m_prev, jnp.max(st, axis=0, keepdims=True))
            alpha = jnp.exp2(m_prev - m_new)
            p = jnp.exp2(st - m_new)
            l_ref[mi] = alpha * l_ref[mi] + jnp.sum(p, axis=0, keepdims=True)
            acc_ref[mi] = alpha * acc_ref[mi] + jnp.dot(vt, p.astype(BF16), preferred_element_type=F32)
            m_ref[mi] = m_new

    scores(0, sa_ref)

    def tile_pair(i, carry):
        j = 2 * i
        scores(j + 1, sb_ref)
        softmax_pv(j, sa_ref, masked=False)
        scores(j + 2, sa_ref)
        softmax_pv(j + 1, sb_ref, masked=False)
        return carry

    lax.fori_loop(0, qi // 2, tile_pair, 0)

    @pl.when(qi % 2 == 0)
    def _():
        softmax_pv(qi, sa_ref, masked=True)

    @pl.when(qi % 2 == 1)
    def _():
        scores(qi, sb_ref)
        softmax_pv(qi - 1, sa_ref, masked=False)
        softmax_pv(qi, sb_ref, masked=True)

    lam = (jnp.exp(jnp.sum(lq1_ref[...] * lk1_ref[...], axis=-1, keepdims=True))
           - jnp.exp(jnp.sum(lq2_ref[...] * lk2_ref[...], axis=-1, keepdims=True))
           + LAMBDA_INIT)
    ot = acc_ref[0] / l_ref[0] - lam * (acc_ref[1] / l_ref[1])
    ot = ot * lax.rsqrt(jnp.mean(ot * ot, axis=0, keepdims=True) + RMS_EPS)
    o_ref[0] = (ot.T * g_ref[...] * (1.0 - LAMBDA_INIT)).astype(o_ref.dtype)


def _attention(proj, lq1, lk1, lq2, lk2, subln_g, *, tq, q_col, k_col, v_col):
    b, s, _ = proj.shape
    w = ATT_V_DIM
    assert s % tq == 0, (s, tq)
    vec = pl.BlockSpec((1, ATT_HEAD_DIM), lambda bi, h, qi: (0, 0))
    return pl.pallas_call(
        functools.partial(_attn_kernel, tq=tq),
        grid=(b, ATT_HEADS, s // tq),
        in_specs=[
            vec, vec, vec, vec,
            pl.BlockSpec((1, w), lambda bi, h, qi: (0, 0)),
            pl.BlockSpec((1, tq, w), lambda bi, h, qi: (bi, qi, q_col + h)),
            pl.BlockSpec((1, s, w), lambda bi, h, qi: (bi, 0, k_col + h)),
            pl.BlockSpec((1, s, w), lambda bi, h, qi: (bi, 0, v_col + h)),
        ],
        out_specs=pl.BlockSpec((1, tq, w), lambda bi, h, qi: (bi, qi, h)),
        out_shape=jax.ShapeDtypeStruct((b, s, ATT_HEADS * w), BF16),
        scratch_shapes=[
            pltpu.VMEM((s // tq, w, tq), BF16),
            pltpu.VMEM((2, 1, tq), F32),
            pltpu.VMEM((2, 1, tq), F32),
            pltpu.VMEM((2, w, tq), F32),
            pltpu.VMEM((2, tq, tq), F32),
            pltpu.VMEM((2, tq, tq), F32),
        ],
        compiler_params=_params("parallel", "parallel", "arbitrary"),
        name="diff_attn",
    )(lq1, lk1, lq2, lk2, subln_g, proj, proj, proj)


def _split3(x):
    p1 = x.astype(BF16)
    r1 = x - p1.astype(F32)
    p2 = r1.astype(BF16)
    p3 = (r1 - p2.astype(F32)).astype(BF16)
    return p1, p2, p3


def _ssd_kernel(dt_ref, dtb_ref, alog_ref, x_ref, b_ref, c_ref, z_ref, dskip_ref, ng_ref,
                o_ref,
                state_ref, dt_all_ref, acs_ref, acst_ref):
    c = pl.program_id(1)
    gblk = pl.program_id(2)
    ln = SSM_CHUNK
    hp = SSM_HEAD_DIM
    ns = SSM_D_STATE
    n_grp = b_ref.shape[-1] // ns
    gw = x_ref.shape[-1] // n_grp
    heads = gw // hp

    row = lax.broadcasted_iota(jnp.int32, (ln, ln), 0)
    col = lax.broadcasted_iota(jnp.int32, (ln, ln), 1)
    tril = col <= row

    @pl.when(gblk == 0)
    def _():
        xb = dt_ref[0] + dtb_ref[...]
        dt_all = jnp.maximum(xb, 0.0) + jnp.log1p(jnp.exp(-jnp.abs(xb)))
        adt = dt_all * (-jnp.exp(alog_ref[...]))
        ltri = jnp.where(tril, 1.0, 0.0).astype(BF16)
        acs_all = sum(jnp.dot(ltri, part, preferred_element_type=F32) for part in _split3(adt))
        dt_all_ref[...] = dt_all
        acs_ref[...] = acs_all
        acst_ref[...] = acs_all.T

    shift = lax.rem(V7X_LANES - n_grp * heads * gblk, V7X_LANES)
    dt_blk = pltpu.roll(dt_all_ref[...], shift, 1)
    acs_blk = pltpu.roll(acs_ref[...], shift, 1)

    lane = lax.broadcasted_iota(jnp.int32, (ln, 2 * hp), 1)
    lo = lane < hp
    hi = jnp.logical_not(lo)

    @pl.when(c == 0)
    def _():
        for gi in range(n_grp):
            state_ref[gblk * n_grp + gi] = jnp.zeros(state_ref.shape[1:], F32)

    for gi in range(n_grp):
        g = gblk * n_grp + gi
        gcols = slice(gi * gw, (gi + 1) * gw)
        scols = slice(gi * ns, (gi + 1) * ns)
        xc = x_ref[0, :, gcols].astype(F32)
        bc = b_ref[0, :, scols]
        cc = c_ref[0, :, scols]
        cb = lax.dot_general(cc, bc, (((1,), (1,)), ((), ())), preferred_element_type=F32)
        state = state_ref[g]
        y_off = jnp.dot(cc, state.astype(BF16), preferred_element_type=F32)

        y_parts = []
        xw_parts = []
        dec_last_parts = []
        for pr in range(heads // 2):
            sl = slice(pr * 2 * hp, (pr + 1) * 2 * hp)
            h0 = gi * heads + 2 * pr
            acs_b = [jnp.broadcast_to(acs_blk[:, h:h + 1], (ln, 2 * hp)) for h in (h0, h0 + 1)]
            dt_pair = jnp.where(lo, dt_blk[:, h0:h0 + 1], dt_blk[:, h0 + 1:h0 + 2])
            acs_pair = jnp.where(lo, acs_b[0], acs_b[1])
            xdt = xc[:, sl] * dt_pair
            y_pair = y_off[:, sl] * jnp.exp(acs_pair)
            for k, keep in enumerate((lo, hi)):
                rowv = acst_ref[pl.ds(g * heads + 2 * pr + k, 1), :]
                decay = jnp.exp(jnp.where(tril, acs_b[k] - rowv, -jnp.inf))
                m = (cb * decay).astype(BF16)
                y_pair = y_pair + jnp.dot(m, jnp.where(keep, xdt, 0.0).astype(BF16),
                                          preferred_element_type=F32)
            y_parts.append(y_pair)
            last = acs_pair[ln - 1:ln, :]
            xw_parts.append((xdt * jnp.exp(last - acs_pair)).astype(BF16))
            dec_last_parts.append(jnp.exp(last))

        xw = jnp.concatenate(xw_parts, axis=1)
        dec_last = jnp.concatenate(dec_last_parts, axis=1)
        bct = bc.astype(F32).T.astype(BF16)
        state_ref[g] = state * dec_last + jnp.dot(bct, xw, preferred_element_type=F32)

        y = jnp.concatenate(y_parts, axis=1)
        y = (y + xc * dskip_ref[:, gcols]) * z_ref[0, :, gcols].astype(F32)
        o_ref[0, :, gcols] = (_rms_scale(y) * ng_ref[:, gcols]).astype(o_ref.dtype)


def _ssd(proj, dt_raw, dt_bias, a_log, d_skip_row, norm_g, *, n_grp, x_col, b_col, c_col, z_col):
    bsz, s, _ = proj.shape
    ln = SSM_CHUNK
    gw = d_skip_row.shape[1] // SSM_GROUPS
    xw, sw = n_grp * gw, n_grp * SSM_D_STATE
    assert s % ln == 0 and gw % (2 * SSM_HEAD_DIM) == 0 and SSM_GROUPS % n_grp == 0, (s, ln, gw)
    assert x_col % xw == 0 and z_col % xw == 0 and b_col % sw == 0 and c_col % sw == 0
    row128 = pl.BlockSpec((1, V7X_LANES), lambda b, c, g: (0, 0))
    return pl.pallas_call(
        _ssd_kernel,
        grid=(bsz, s // ln, SSM_GROUPS // n_grp),
        in_specs=[
            pl.BlockSpec((1, ln, V7X_LANES), lambda b, c, g: (b, c, 0)),
            row128, row128,
            pl.BlockSpec((1, ln, xw), lambda b, c, g: (b, c, x_col // xw + g)),
            pl.BlockSpec((1, ln, sw), lambda b, c, g: (b, c, b_col // sw + g)),
            pl.BlockSpec((1, ln, sw), lambda b, c, g: (b, c, c_col // sw + g)),
            pl.BlockSpec((1, ln, xw), lambda b, c, g: (b, c, z_col // xw + g)),
            pl.BlockSpec((1, xw), lambda b, c, g: (0, g)),
            pl.BlockSpec((1, xw), lambda b, c, g: (0, g)),
        ],
        out_specs=pl.BlockSpec((1, ln, xw), lambda b, c, g: (b, c, g)),
        out_shape=jax.ShapeDtypeStruct((bsz, s, SSM_GROUPS * gw), BF16),
        scratch_shapes=[
            pltpu.VMEM((SSM_GROUPS, SSM_D_STATE, gw), F32),
            pltpu.VMEM((ln, V7X_LANES), F32),
            pltpu.VMEM((ln, V7X_LANES), F32),
            pltpu.VMEM((V7X_LANES, ln), F32),
        ],
        compiler_params=_params("parallel", "arbitrary", "arbitrary"),
        name="ssd",
    )(dt_raw, dt_bias, a_log, proj, proj, proj, proj, d_skip_row, norm_g)


def _merge_kernel(a_ref, s_ref, wa_ref, ws_ref, ga_ref, gs_ref, out_ref):
    ya = jnp.dot(a_ref[...], wa_ref[...], preferred_element_type=F32)
    ys = jnp.dot(s_ref[...], ws_ref[...], preferred_element_type=F32)
    out_ref[...] = (ga_ref[...].astype(F32) * ya + gs_ref[...].astype(F32) * ys).astype(out_ref.dtype)


def _merge(o_att, o_ssm, w_att, w_ssm, proj, *, tm, tn, ga_col, gs_col):
    t, ka = o_att.shape
    ks = o_ssm.shape[1]
    n = w_att.shape[1]
    assert t % tm == 0 and n % tn == 0, (t, tm, n, tn)
    return pl.pallas_call(
        _merge_kernel,
        grid=(t // tm, n // tn),
        in_specs=[
            pl.BlockSpec((tm, ka), lambda i, j: (i, 0)),
            pl.BlockSpec((tm, ks), lambda i, j: (i, 0)),
            pl.BlockSpec((ka, tn), lambda i, j: (0, j)),
            pl.BlockSpec((ks, tn), lambda i, j: (0, j)),
            pl.BlockSpec((tm, tn), lambda i, j: (i, ga_col + j)),
            pl.BlockSpec((tm, tn), lambda i, j: (i, gs_col + j)),
        ],
        out_specs=pl.BlockSpec((tm, tn), lambda i, j: (i, j)),
        out_shape=jax.ShapeDtypeStruct((t, n), BF16),
        compiler_params=_params("parallel", "arbitrary"),
        name="merge",
    )(o_att, o_ssm, w_att, w_ssm, proj, proj)


def _outproj_kernel(m_ref, w_ref, h_ref, g_ref, out_ref):
    y = jnp.dot(m_ref[...], w_ref[...], preferred_element_type=F32)
    out_ref[...] = h_ref[...] + _rms_scale(y) * g_ref[...]


def _outproj(merged, w_out, h, post_g, *, tm):
    t, d = h.shape
    k = merged.shape[1]
    assert t % tm == 0, (t, tm)
    return pl.pallas_call(
        _outproj_kernel,
        grid=(t // tm,),
        in_specs=[
            pl.BlockSpec((tm, k), lambda i: (i, 0)),
            pl.BlockSpec((k, d), lambda i: (0, 0)),
            pl.BlockSpec((tm, d), lambda i: (i, 0)),
            pl.BlockSpec((1, d), lambda i: (0, 0)),
        ],
        out_specs=pl.BlockSpec((tm, d), lambda i: (i, 0)),
        out_shape=jax.ShapeDtypeStruct((t, d), F32),
        compiler_params=_params("parallel"),
        name="out_proj",
    )(merged, w_out, h, post_g)


def _pad_lanes(v):
    return jnp.pad(v.astype(F32), (0, V7X_LANES - v.shape[0]))[None, :]


def kernel(x, ffn1_pre_g, ffn1_w_gate, ffn1_w_up, ffn1_w_down, ffn1_post_g, mix_pre_g, w_in, att_lambda_q1, att_lambda_k1, att_lambda_q2, att_lambda_k2, att_subln_g, ssm_conv_w, ssm_conv_b, ssm_dt_bias, ssm_a_log, ssm_d, ssm_norm_g, w_branch_att, w_branch_ssm, w_out, mix_post_g, ffn2_pre_g, ffn2_w_gate, ffn2_w_up, ffn2_w_down, ffn2_post_g):
    bsz, seq, d = x.shape
    t = bsz * seq
    qk_w = ATT_HEADS * 2 * ATT_HEAD_DIM
    v_w = ATT_HEADS * ATT_V_DIM
    d_inner = ssm_norm_g.shape[1]
    n_heads = ssm_dt_bias.shape[1]
    bc_w = SSM_GROUPS * SSM_D_STATE
    conv_dim = d_inner + 2 * bc_w

    w = w_in[0].astype(BF16)
    o_q, o_k, o_v = 0, qk_w, 2 * qk_w
    o_z = o_v + v_w
    o_xbc = o_z + d_inner
    o_dt = o_xbc + conv_dim
    o_gate = o_dt + n_heads
    w_g = w[:, o_gate:]
    w_dt = jnp.pad(w[:, o_dt:o_gate], ((0, 0), (0, V7X_LANES - n_heads)))
    c_q, c_k, c_v, c_z, c_x = o_q, o_k, o_v, o_z, o_xbc
    c_b = c_x + d_inner
    c_c = c_b + bc_w
    c_ga = c_x + conv_dim
    c_gs = c_ga + d

    h = x.reshape(t, d)
    h = _ffn(h, ffn1_pre_g, ffn1_w_gate[0].astype(BF16), ffn1_w_up[0].astype(BF16),
             ffn1_w_down[0].astype(BF16), ffn1_post_g, tm=512, tf=512)

    proj, dt_raw = _inproj(h, mix_pre_g, w, w_g, w_dt, ssm_conv_w[0], ssm_conv_b,
                           tm=1024, tn=1024, seq=seq, qk_w=qk_w, kv_end=o_z, z_end=o_xbc,
                           a_cols=o_dt)
    proj3 = proj.reshape(bsz, seq, proj.shape[1])

    o_att = _attention(proj3, att_lambda_q1, att_lambda_k1, att_lambda_q2, att_lambda_k2,
                       att_subln_g, tq=512,
                       q_col=c_q // ATT_V_DIM, k_col=c_k // ATT_V_DIM, v_col=c_v // ATT_V_DIM)

    d_skip_row = jnp.repeat(ssm_d[0].astype(F32), SSM_HEAD_DIM)[None, :]
    o_ssm = _ssd(proj3, dt_raw.reshape(bsz, seq, V7X_LANES), _pad_lanes(ssm_dt_bias[0]),
                 _pad_lanes(ssm_a_log[0]), d_skip_row, ssm_norm_g,
                 n_grp=4, x_col=c_x, b_col=c_b, c_col=c_c, z_col=c_z)

    tn_merge = 512
    merged = _merge(o_att.reshape(t, v_w), o_ssm.reshape(t, d_inner),
                    w_branch_att[0].astype(BF16), w_branch_ssm[0].astype(BF16), proj,
                    tm=1024, tn=tn_merge, ga_col=c_ga // tn_merge, gs_col=c_gs // tn_merge)
    h = _outproj(merged, w_out[0].astype(BF16), h, mix_post_g, tm=512)

    h = _ffn(h, ffn2_pre_g, ffn2_w_gate[0].astype(BF16), ffn2_w_up[0].astype(BF16),
             ffn2_w_down[0].astype(BF16), ffn2_post_g, tm=512, tf=512)
    return h.reshape(bsz, seq, d)
```

```python
import functools
import math

import jax
import jax.numpy as jnp
from jax import lax
from jax.experimental import pallas as pl
from jax.experimental.pallas import tpu as pltpu

F32 = jnp.float32
BF16 = jnp.bfloat16

V7X_SUBLANES = 8
V7X_LANES = 128
V7X_VMEM_LIMIT_BYTES = 56 * 1024 * 1024

RMS_EPS = 1e-6

ATT_HEADS = 8
ATT_HEAD_DIM = 128
ATT_V_DIM = 2 * ATT_HEAD_DIM
SSM_HEAD_DIM = 64
SSM_GROUPS = 8
SSM_D_STATE = 128
SSM_CONV = 4
SSM_CHUNK = 128
LAMBDA_INIT = 0.8 - 0.6 * math.exp(-0.3 * 0)
LOG2_E = math.log2(math.e)

FFN_DOWN_CHUNK = 512
INPROJ_CHUNK = 256
INPROJ_CONV_ROWS = 128


def _params(*semantics):
    return pltpu.CompilerParams(dimension_semantics=semantics,
                                vmem_limit_bytes=V7X_VMEM_LIMIT_BYTES)


def _rms_scale(x):
    return x * lax.rsqrt(jnp.mean(x * x, axis=-1, keepdims=True) + RMS_EPS)


def _silu(x):
    return x / (1.0 + jnp.exp(-x))


def _sigmoid(x):
    return 1.0 / (1.0 + jnp.exp(-x))


def _ffn_kernel(h_ref, pre_g_ref, wg_ref, wu_ref, wd_ref, post_g_ref, out_ref, xn_ref, acc_ref):
    j = pl.program_id(1)

    @pl.when(j == 0)
    def _():
        xn_ref[...] = (_rms_scale(h_ref[...]) * pre_g_ref[...]).astype(BF16)
        acc_ref[...] = jnp.zeros(acc_ref.shape, F32)

    xn = xn_ref[...]
    gate = jnp.dot(xn, wg_ref[...], preferred_element_type=F32)
    up = jnp.dot(xn, wu_ref[...], preferred_element_type=F32)
    act = (_silu(gate) * up).astype(BF16)
    d = acc_ref.shape[1]
    for c in range(d // FFN_DOWN_CHUNK):
        cols = slice(c * FFN_DOWN_CHUNK, (c + 1) * FFN_DOWN_CHUNK)
        acc_ref[:, cols] += jnp.dot(act, wd_ref[:, cols], preferred_element_type=F32)

    @pl.when(j == pl.num_programs(1) - 1)
    def _():
        out_ref[...] = h_ref[...] + 0.5 * (_rms_scale(acc_ref[...]) * post_g_ref[...])


def _ffn(h, pre_g, w_gate, w_up, w_down, post_g, *, tm, tf):
    t, d = h.shape
    f = w_gate.shape[1]
    assert t % tm == 0 and f % tf == 0, (t, tm, f, tf)
    return pl.pallas_call(
        _ffn_kernel,
        grid=(t // tm, f // tf),
        in_specs=[
            pl.BlockSpec((tm, d), lambda i, j: (i, 0)),
            pl.BlockSpec((1, d), lambda i, j: (0, 0)),
            pl.BlockSpec((d, tf), lambda i, j: (0, j)),
            pl.BlockSpec((d, tf), lambda i, j: (0, j)),
            pl.BlockSpec((tf, d), lambda i, j: (j, 0)),
            pl.BlockSpec((1, d), lambda i, j: (0, 0)),
        ],
        out_specs=pl.BlockSpec((tm, d), lambda i, j: (i, 0)),
        out_shape=jax.ShapeDtypeStruct((t, d), F32),
        scratch_shapes=[pltpu.VMEM((tm, d), BF16), pltpu.VMEM((tm, d), F32)],
        compiler_params=_params("parallel", "arbitrary"),
        name="ffn",
    )(h, pre_g, w_gate, w_up, w_down, post_g)


def _inproj_kernel(h_ref, g_ref, wa_ref, wg_ref, wdt_ref, cw_ref, cb_ref, out_ref, dt_ref,
                   xn_ref, tail_ref, *, j_k, j_z, j_x, j_g, tiles_per_seq):
    i = pl.program_id(0)
    j = pl.program_id(1)
    tm, tn = out_ref.shape
    tail = V7X_SUBLANES

    @pl.when(j == 0)
    def _():
        xn = (_rms_scale(h_ref[...]) * g_ref[...]).astype(BF16)
        xn_ref[...] = xn
        dt_ref[...] = jnp.dot(xn, wdt_ref[...], preferred_element_type=F32)

        @pl.when(i == 0)
        def _():
            tail_ref[...] = jnp.zeros(tail_ref.shape, F32)

    def project(w_ref, epilogue):
        xn = xn_ref[...]
        for c in range(tn // INPROJ_CHUNK):
            cols = slice(c * INPROJ_CHUNK, (c + 1) * INPROJ_CHUNK)
            epilogue(jnp.dot(xn, w_ref[:, cols], preferred_element_type=F32), cols)

    def elementwise(fn):
        def epilogue(y, cols):
            out_ref[:, cols] = fn(y).astype(out_ref.dtype)
        return epilogue

    def conv_silu(y, cols):
        jx = j - j_x
        rb = INPROJ_CONV_ROWS
        sub = lax.broadcasted_iota(jnp.int32, (rb, INPROJ_CHUNK), 0) % tail
        for r in range(tm // rb):
            r0 = r * rb
            cur = y[r0:r0 + rb]
            if r == 0:
                prev = jnp.concatenate([tail_ref[jx, :, cols], y[0:rb - tail]], axis=0)
            else:
                prev = y[r0 - tail:r0 - tail + rb]
            acc = cb_ref[:, cols] + cw_ref[SSM_CONV - 1:SSM_CONV, cols] * cur
            for s in range(1, SSM_CONV):
                k = SSM_CONV - 1 - s
                merged = jnp.where(sub >= tail - s, prev, cur).reshape(-1, tail, INPROJ_CHUNK)
                shifted = pltpu.roll(merged, s, 1).reshape(rb, INPROJ_CHUNK)
                acc = acc + cw_ref[k:k + 1, cols] * shifted
            out_ref[r0:r0 + rb, cols] = _silu(acc).astype(out_ref.dtype)
        seq_ends = (i % tiles_per_seq) == tiles_per_seq - 1
        tail_ref[jx, :, cols] = jnp.where(seq_ends, 0.0, y[tm - tail:tm])

    @pl.when(j < j_k)
    def _():
        project(wa_ref, elementwise(lambda y: y * (LOG2_E * ATT_HEAD_DIM ** -0.5)))

    @pl.when(jnp.logical_and(j >= j_k, j < j_z))
    def _():
        project(wa_ref, elementwise(lambda y: y))

    @pl.when(jnp.logical_and(j >= j_z, j < j_x))
    def _():
        project(wa_ref, elementwise(_silu))

    @pl.when(jnp.logical_and(j >= j_x, j < j_g))
    def _():
        project(wa_ref, conv_silu)

    @pl.when(j >= j_g)
    def _():
        project(wg_ref, elementwise(_sigmoid))


def _inproj(h, g, w_all, w_g, w_dt, conv_w, conv_b, *, tm, tn, seq, qk_w, kv_end, z_end, a_cols):
    t, d = h.shape
    na, ng = a_cols, w_g.shape[1]
    ndt = w_dt.shape[1]
    conv_dim = conv_w.shape[1]
    assert t % tm == 0 and seq % tm == 0, (t, seq, tm)
    assert all(v % tn == 0 for v in (qk_w, kv_end, z_end, na, ng, conv_dim)), (tn,)
    assert z_end + conv_dim == na and tn % INPROJ_CHUNK == 0 and na <= w_all.shape[1]
    j_k, j_z, j_x, j_g = qk_w // tn, kv_end // tn, z_end // tn, na // tn
    n_conv = conv_dim // tn
    return pl.pallas_call(
        functools.partial(_inproj_kernel, j_k=j_k, j_z=j_z, j_x=j_x, j_g=j_g,
                          tiles_per_seq=seq // tm),
        grid=(t // tm, (na + ng) // tn),
        in_specs=[
            pl.BlockSpec((tm, d), lambda i, j: (i, 0)),
            pl.BlockSpec((1, d), lambda i, j: (0, 0)),
            pl.BlockSpec((d, tn), lambda i, j: (0, jnp.minimum(j, j_g - 1))),
            pl.BlockSpec((d, tn), lambda i, j: (0, jnp.maximum(j - j_g, 0))),
            pl.BlockSpec((d, ndt), lambda i, j: (0, 0)),
            pl.BlockSpec((SSM_CONV, tn), lambda i, j: (0, jnp.clip(j - j_x, 0, n_conv - 1))),
            pl.BlockSpec((1, tn), lambda i, j: (0, jnp.clip(j - j_x, 0, n_conv - 1))),
        ],
        out_specs=[
            pl.BlockSpec((tm, tn), lambda i, j: (i, j)),
            pl.BlockSpec((tm, ndt), lambda i, j: (i, 0)),
        ],
        out_shape=[jax.ShapeDtypeStruct((t, na + ng), BF16), jax.ShapeDtypeStruct((t, ndt), F32)],
        scratch_shapes=[
            pltpu.VMEM((tm, d), BF16),
            pltpu.VMEM((n_conv, V7X_SUBLANES, tn), F32),
        ],
        compiler_params=_params("arbitrary", "arbitrary"),
        name="in_proj",
    )(h, g, w_all, w_g, w_dt, conv_w, conv_b)


def _attn_kernel(lq1_ref, lk1_ref, lq2_ref, lk2_ref, g_ref, q_ref, k_ref, v_ref, o_ref,
                 vt_ref, m_ref, l_ref, acc_ref, sa_ref, sb_ref, *, tq):
    qi = pl.program_id(2)
    dh = ATT_HEAD_DIM
    n_kv = vt_ref.shape[0]

    @pl.when(qi == 0)
    def _():
        for j in range(n_kv):
            vt_ref[j] = v_ref[0, j * tq:(j + 1) * tq, :].T

    m_ref[...] = jnp.full(m_ref.shape, -jnp.inf, F32)
    l_ref[...] = jnp.zeros(l_ref.shape, F32)
    acc_ref[...] = jnp.zeros(acc_ref.shape, F32)

    def scores(j, s_ref):
        start = pl.multiple_of(j * tq, tq)
        for mi in range(2):
            q = q_ref[0, :, mi * dh:(mi + 1) * dh]
            k = k_ref[0, pl.ds(start, tq), mi * dh:(mi + 1) * dh]
            s_ref[mi] = lax.dot_general(k, q, (((1,), (1,)), ((), ())), preferred_element_type=F32)

    def softmax_pv(j, s_ref, masked):
        vt = vt_ref[j]
        for mi in range(2):
            st = s_ref[mi]
            if masked:
                kpos = lax.broadcasted_iota(jnp.int32, st.shape, 0)
                qpos = lax.broadcasted_iota(jnp.int32, st.shape, 1)
                st = jnp.where(kpos <= qpos, st, -jnp.inf)
            m_prev = m_ref[mi]
            m_new = jnp.maximum(m_prev, jnp.max(st, axis=0, keepdims=True))
            alpha = jnp.exp2(m_prev - m_new)
            p = jnp.exp2(st - m_new)
            l_ref[mi] = alpha * l_ref[mi] + jnp.sum(p, axis=0, keepdims=True)
            acc_ref[mi] = alpha * acc_ref[mi] + jnp.dot(vt, p.astype(BF16), preferred_element_type=F32)
            m_ref[mi] = m_new

    scores(0, sa_ref)

    def tile_pair(i, carry):
        j = 2 * i
        scores(j + 1, sb_ref)
        softmax_pv(j, sa_ref, masked=False)
        scores(j + 2, sa_ref)
        softmax_pv(j + 1, sb_ref, masked=False)
        return carry

    lax.fori_loop(0, qi // 2, tile_pair, 0)

    def finalize():
        lam = (jnp.exp(jnp.sum(lq1_ref[...] * lk1_ref[...], axis=-1, keepdims=True))
               - jnp.exp(jnp.sum(lq2_ref[...] * lk2_ref[...], axis=-1, keepdims=True))
               + LAMBDA_INIT)
        ot = acc_ref[0] / l_ref[0] - lam * (acc_ref[1] / l_ref[1])
        ot = ot * lax.rsqrt(jnp.mean(ot * ot, axis=0, keepdims=True) + RMS_EPS)
        o_ref[0] = (ot.T * g_ref[...] * (1.0 - LAMBDA_INIT)).astype(o_ref.dtype)

    @pl.when(qi % 2 == 0)
    def _():
        softmax_pv(qi, sa_ref, masked=True)
        finalize()

    @pl.when(qi % 2 == 1)
    def _():
        scores(qi, sb_ref)
        softmax_pv(qi - 1, sa_ref, masked=False)
        softmax_pv(qi, sb_ref, masked=True)
        finalize()


def _attention(proj, lq1, lk1, lq2, lk2, subln_g, *, tq, q_col, k_col, v_col):
    b, s, _ = proj.shape
    w = ATT_V_DIM
    assert s % tq == 0, (s, tq)
    vec = pl.BlockSpec((1, ATT_HEAD_DIM), lambda bi, h, qi: (0, 0))
    return pl.pallas_call(
        functools.partial(_attn_kernel, tq=tq),
        grid=(b, ATT_HEADS, s // tq),
        in_specs=[
            vec, vec, vec, vec,
            pl.BlockSpec((1, w), lambda bi, h, qi: (0, 0)),
            pl.BlockSpec((1, tq, w), lambda bi, h, qi: (bi, qi, q_col + h)),
            pl.BlockSpec((1, s, w), lambda bi, h, qi: (bi, 0, k_col + h)),
            pl.BlockSpec((1, s, w), lambda bi, h, qi: (bi, 0, v_col + h)),
        ],
        out_specs=pl.BlockSpec((1, tq, w), lambda bi, h, qi: (bi, qi, h)),
        out_shape=jax.ShapeDtypeStruct((b, s, ATT_HEADS * w), BF16),
        scratch_shapes=[
            pltpu.VMEM((s // tq, w, tq), BF16),
            pltpu.VMEM((2, 1, tq), F32),
            pltpu.VMEM((2, 1, tq), F32),
            pltpu.VMEM((2, w, tq), F32),
            pltpu.VMEM((2, tq, tq), F32),
            pltpu.VMEM((2, tq, tq), F32),
        ],
        compiler_params=_params("parallel", "parallel", "arbitrary"),
        name="diff_attn",
    )(lq1, lk1, lq2, lk2, subln_g, proj, proj, proj)


def _split3(x):
    p1 = x.astype(BF16)
    r1 = x - p1.astype(F32)
    p2 = r1.astype(BF16)
    p3 = (r1 - p2.astype(F32)).astype(BF16)
    return p1, p2, p3


def _ssd_kernel(dt_ref, dtb_ref, alog_ref, *refs, n_parts):
    x_refs, (b_ref, c_ref) = refs[:n_parts], refs[n_parts:n_parts + 2]
    z_refs = refs[n_parts + 2:2 * n_parts + 2]
    dskip_ref, ng_ref, o_ref, state_ref = refs[2 * n_parts + 2:]
    c = pl.program_id(1)
    ln = SSM_CHUNK
    hp = SSM_HEAD_DIM
    ns = SSM_D_STATE
    n_grp = b_ref.shape[-1] // ns
    gw = o_ref.shape[-1] // n_grp
    heads = gw // hp
    part_grp = n_grp // n_parts

    row = lax.broadcasted_iota(jnp.int32, (ln, ln), 0)
    col = lax.broadcasted_iota(jnp.int32, (ln, ln), 1)
    tril = col <= row

    xb = dt_ref[0] + dtb_ref[...]
    dt_all = jnp.maximum(xb, 0.0) + jnp.log1p(jnp.exp(-jnp.abs(xb)))
    adt = dt_all * (-jnp.exp(alog_ref[...]))
    ltri = jnp.where(tril, 1.0, 0.0).astype(BF16)
    acs_all = sum(jnp.dot(ltri, part, preferred_element_type=F32) for part in _split3(adt))
    acst = acs_all.T

    lane = lax.broadcasted_iota(jnp.int32, (ln, 2 * hp), 1)
    lo = lane < hp
    hi = jnp.logical_not(lo)

    @pl.when(c == 0)
    def _():
        state_ref[...] = jnp.zeros(state_ref.shape, F32)

    for g in range(n_grp):
        gcols = slice(g * gw, (g + 1) * gw)
        pcols = slice((g % part_grp) * gw, (g % part_grp + 1) * gw)
        scols = slice(g * ns, (g + 1) * ns)
        xc = x_refs[g // part_grp][0, :, pcols].astype(F32)
        bc = b_ref[0, :, scols]
        cc = c_ref[0, :, scols]
        cb = lax.dot_general(cc, bc, (((1,), (1,)), ((), ())), preferred_element_type=F32)
        state = state_ref[g]
        y_off = jnp.dot(cc, state.astype(BF16), preferred_element_type=F32)

        y_parts = []
        xw_parts = []
        dec_last_parts = []
        for pr in range(heads // 2):
            sl = slice(pr * 2 * hp, (pr + 1) * 2 * hp)
            h0 = g * heads + 2 * pr
            acs_b = [jnp.broadcast_to(acs_all[:, h:h + 1], (ln, 2 * hp)) for h in (h0, h0 + 1)]
            dt_pair = jnp.where(lo, dt_all[:, h0:h0 + 1], dt_all[:, h0 + 1:h0 + 2])
            acs_pair = jnp.where(lo, acs_b[0], acs_b[1])
            xdt = xc[:, sl] * dt_pair
            y_pair = y_off[:, sl] * jnp.exp(acs_pair)
            for k, keep in enumerate((lo, hi)):
                rowv = acst[h0 + k:h0 + k + 1, :]
                decay = jnp.exp(jnp.where(tril, acs_b[k] - rowv, -jnp.inf))
                m = (cb * decay).astype(BF16)
                y_pair = y_pair + jnp.dot(m, jnp.where(keep, xdt, 0.0).astype(BF16),
                                          preferred_element_type=F32)
            y_parts.append(y_pair)
            last = acs_pair[ln - 1:ln, :]
            xw_parts.append((xdt * jnp.exp(last - acs_pair)).astype(BF16))
            dec_last_parts.append(jnp.exp(last))

        xw = jnp.concatenate(xw_parts, axis=1)
        dec_last = jnp.concatenate(dec_last_parts, axis=1)
        bct = bc.astype(F32).T.astype(BF16)
        state_ref[g] = state * dec_last + jnp.dot(bct, xw, preferred_element_type=F32)

        y = jnp.concatenate(y_parts, axis=1)
        y = (y + xc * dskip_ref[:, gcols]) * z_refs[g // part_grp][0, :, pcols].astype(F32)
        o_ref[0, :, gcols] = (_rms_scale(y) * ng_ref[:, gcols]).astype(o_ref.dtype)


def _ssd(proj, dt_raw, dt_bias, a_log, d_skip_row, norm_g, *, n_parts, x_col, b_col, c_col, z_col):
    bsz, s, _ = proj.shape
    ln = SSM_CHUNK
    d_inner = d_skip_row.shape[1]
    gw = d_inner // SSM_GROUPS
    xw, sw = d_inner // n_parts, SSM_GROUPS * SSM_D_STATE
    assert s % ln == 0 and gw % (2 * SSM_HEAD_DIM) == 0 and SSM_GROUPS % n_parts == 0, (s, ln, gw)
    assert x_col % xw == 0 and z_col % xw == 0 and b_col % sw == 0 and c_col % sw == 0

    def part(col, p):
        return pl.BlockSpec((1, ln, xw), lambda b, c: (b, c, col // xw + p))

    row128 = pl.BlockSpec((1, V7X_LANES), lambda b, c: (0, 0))
    return pl.pallas_call(
        functools.partial(_ssd_kernel, n_parts=n_parts),
        grid=(bsz, s // ln),
        in_specs=[
            pl.BlockSpec((1, ln, V7X_LANES), lambda b, c: (b, c, 0)),
            row128, row128,
            *[part(x_col, p) for p in range(n_parts)],
            pl.BlockSpec((1, ln, sw), lambda b, c: (b, c, b_col // sw)),
            pl.BlockSpec((1, ln, sw), lambda b, c: (b, c, c_col // sw)),
            *[part(z_col, p) for p in range(n_parts)],
            pl.BlockSpec((1, d_inner), lambda b, c: (0, 0)),
            pl.BlockSpec((1, d_inner), lambda b, c: (0, 0)),
        ],
        out_specs=pl.BlockSpec((1, ln, d_inner), lambda b, c: (b, c, 0)),
        out_shape=jax.ShapeDtypeStruct((bsz, s, d_inner), BF16),
        scratch_shapes=[pltpu.VMEM((SSM_GROUPS, SSM_D_STATE, gw), F32)],
        compiler_params=_params("parallel", "arbitrary"),
        name="ssd",
    )(dt_raw, dt_bias, a_log, *([proj] * n_parts), proj, proj, *([proj] * n_parts),
      d_skip_row, norm_g)


def _merge_kernel(a_ref, s_ref, wa_ref, ws_ref, ga_ref, gs_ref, out_ref):
    ya = jnp.dot(a_ref[...], wa_ref[...], preferred_element_type=F32)
    ys = jnp.dot(s_ref[...], ws_ref[...], preferred_element_type=F32)
    out_ref[...] = (ga_ref[...].astype(F32) * ya + gs_ref[...].astype(F32) * ys).astype(out_ref.dtype)


def _merge(o_att, o_ssm, w_att, w_ssm, proj, *, tm, tn, ga_col, gs_col):
    t, ka = o_att.shape
    ks = o_ssm.shape[1]
    n = w_att.shape[1]
    assert t % tm == 0 and n % tn == 0, (t, tm, n, tn)
    return pl.pallas_call(
        _merge_kernel,
        grid=(t // tm, n // tn),
        in_specs=[
            pl.BlockSpec((tm, ka), lambda i, j: (i, 0)),
            pl.BlockSpec((tm, ks), lambda i, j: (i, 0)),
            pl.BlockSpec((ka, tn), lambda i, j: (0, j)),
            pl.BlockSpec((ks, tn), lambda i, j: (0, j)),
            pl.BlockSpec((tm, tn), lambda i, j: (i, ga_col + j)),
            pl.BlockSpec((tm, tn), lambda i, j: (i, gs_col + j)),
        ],
        out_specs=pl.BlockSpec((tm, tn), lambda i, j: (i, j)),
        out_shape=jax.ShapeDtypeStruct((t, n), BF16),
        compiler_params=_params("parallel", "arbitrary"),
        name="merge",
    )(o_att, o_ssm, w_att, w_ssm, proj, proj)


def _outproj_kernel(m_ref, w_ref, h_ref, g_ref, out_ref):
    y = jnp.dot(m_ref[...], w_ref[...], preferred_element_type=F32)
    out_ref[...] = h_ref[...] + _rms_scale(y) * g_ref[...]


def _outproj(merged, w_out, h, post_g, *, tm):
    t, d = h.shape
    k = merged.shape[1]
    assert t % tm == 0, (t, tm)
    return pl.pallas_call(
        _outproj_kernel,
        grid=(t // tm,),
        in_specs=[
            pl.BlockSpec((tm, k), lambda i: (i, 0)),
            pl.BlockSpec((k, d), lambda i: (0, 0)),
            pl.BlockSpec((tm, d), lambda i: (i, 0)),
            pl.BlockSpec((1, d), lambda i: (0, 0)),
        ],
        out_specs=pl.BlockSpec((tm, d), lambda i: (i, 0)),
        out_shape=jax.ShapeDtypeStruct((t, d), F32),
        compiler_params=_params("parallel"),
        name="out_proj",
    )(merged, w_out, h, post_g)


def _pad_lanes(v):
    return jnp.pad(v.astype(F32), (0, V7X_LANES - v.shape[0]))[None, :]


def kernel(x, ffn1_pre_g, ffn1_w_gate, ffn1_w_up, ffn1_w_down, ffn1_post_g, mix_pre_g, w_in, att_lambda_q1, att_lambda_k1, att_lambda_q2, att_lambda_k2, att_subln_g, ssm_conv_w, ssm_conv_b, ssm_dt_bias, ssm_a_log, ssm_d, ssm_norm_g, w_branch_att, w_branch_ssm, w_out, mix_post_g, ffn2_pre_g, ffn2_w_gate, ffn2_w_up, ffn2_w_down, ffn2_post_g):
    bsz, seq, d = x.shape
    t = bsz * seq
    qk_w = ATT_HEADS * 2 * ATT_HEAD_DIM
    v_w = ATT_HEADS * ATT_V_DIM
    d_inner = ssm_norm_g.shape[1]
    n_heads = ssm_dt_bias.shape[1]
    bc_w = SSM_GROUPS * SSM_D_STATE
    conv_dim = d_inner + 2 * bc_w

    w = w_in[0].astype(BF16)
    o_q, o_k, o_v = 0, qk_w, 2 * qk_w
    o_z = o_v + v_w
    o_xbc = o_z + d_inner
    o_dt = o_xbc + conv_dim
    o_gate = o_dt + n_heads
    w_g = w[:, o_gate:]
    w_dt = jnp.pad(w[:, o_dt:o_gate], ((0, 0), (0, V7X_LANES - n_heads)))
    c_q, c_k, c_v, c_z, c_x = o_q, o_k, o_v, o_z, o_xbc
    c_b = c_x + d_inner
    c_c = c_b + bc_w
    c_ga = c_x + conv_dim
    c_gs = c_ga + d

    h = x.reshape(t, d)
    h = _ffn(h, ffn1_pre_g, ffn1_w_gate[0].astype(BF16), ffn1_w_up[0].astype(BF16),
             ffn1_w_down[0].astype(BF16), ffn1_post_g, tm=512, tf=512)

    proj, dt_raw = _inproj(h, mix_pre_g, w, w_g, w_dt, ssm_conv_w[0], ssm_conv_b,
                           tm=1024, tn=1024, seq=seq, qk_w=qk_w, kv_end=o_z, z_end=o_xbc,
                           a_cols=o_dt)
    proj3 = proj.reshape(bsz, seq, proj.shape[1])

    o_att = _attention(proj3, att_lambda_q1, att_lambda_k1, att_lambda_q2, att_lambda_k2,
                       att_subln_g, tq=512,
                       q_col=c_q // ATT_V_DIM, k_col=c_k // ATT_V_DIM, v_col=c_v // ATT_V_DIM)

    d_skip_row = jnp.repeat(ssm_d[0].astype(F32), SSM_HEAD_DIM)[None, :]
    o_ssm = _ssd(proj3, dt_raw.reshape(bsz, seq, V7X_LANES), _pad_lanes(ssm_dt_bias[0]),
                 _pad_lanes(ssm_a_log[0]), d_skip_row, ssm_norm_g,
                 n_parts=2, x_col=c_x, b_col=c_b, c_col=c_c, z_col=c_z)

    tn_merge = 512
    merged = _merge(o_att.reshape(t, v_w), o_ssm.reshape(t, d_inner),
                    w_branch_att[0].astype(BF16), w_branch_ssm[0].astype(BF16), proj,
                    tm=1024, tn=tn_merge, ga_col=c_ga // tn_merge, gs_col=c_gs // tn_merge)
    h = _outproj(merged, w_out[0].astype(BF16), h, mix_post_g, tm=512)

    h = _ffn(h, ffn2_pre_g, ffn2_w_gate[0].astype(BF16), ffn2_w_up[0].astype(BF16),
             ffn2_w_down[0].astype(BF16), ffn2_post_g, tm=512, tf=512)
    return h.reshape(bsz, seq, d)
```

```python
import functools
import math

import jax
import jax.numpy as jnp
from jax import lax
from jax.experimental import pallas as pl
from jax.experimental.pallas import tpu as pltpu

F32 = jnp.float32
BF16 = jnp.bfloat16

V7X_SUBLANES = 8
V7X_LANES = 128
V7X_VMEM_LIMIT_BYTES = 56 * 1024 * 1024

RMS_EPS = 1e-6

ATT_HEADS = 8
ATT_HEAD_DIM = 128
ATT_V_DIM = 2 * ATT_HEAD_DIM
SSM_HEAD_DIM = 64
SSM_GROUPS = 8
SSM_D_STATE = 128
SSM_CONV = 4
SSM_CHUNK = 128
LAMBDA_INIT = 0.8 - 0.6 * math.exp(-0.3 * 0)
LOG2_E = math.log2(math.e)

FFN_TM, FFN_TF = 512, 512
INPROJ_TM, INPROJ_TN = 1024, 1024
ATTN_TQ = 512
SSD_COL_PARTS = 2
MERGE_TM, MERGE_TN = 1024, 512
OUTPROJ_TM = 512

FFN_DOWN_CHUNK = 512
INPROJ_CHUNK = 256
INPROJ_CONV_ROWS = 128


def _params(*semantics):
    return pltpu.CompilerParams(dimension_semantics=semantics,
                                vmem_limit_bytes=V7X_VMEM_LIMIT_BYTES)


def _rms_scale(x):
    return x * lax.rsqrt(jnp.mean(x * x, axis=-1, keepdims=True) + RMS_EPS)


def _silu(x):
    return x / (1.0 + jnp.exp(-x))


def _sigmoid(x):
    return 1.0 / (1.0 + jnp.exp(-x))


def _ffn_kernel(h_ref, pre_g_ref, wg_ref, wu_ref, wd_ref, post_g_ref, out_ref, xn_ref, acc_ref):
    j = pl.program_id(1)

    def swiglu_step(first):
        xn = xn_ref[...]
        gate = jnp.dot(xn, wg_ref[...], preferred_element_type=F32)
        up = jnp.dot(xn, wu_ref[...], preferred_element_type=F32)
        act = (_silu(gate) * up).astype(BF16)
        d = acc_ref.shape[1]
        for c in range(d // FFN_DOWN_CHUNK):
            cols = slice(c * FFN_DOWN_CHUNK, (c + 1) * FFN_DOWN_CHUNK)
            part = jnp.dot(act, wd_ref[:, cols], preferred_element_type=F32)
            if first:
                acc_ref[:, cols] = part
            else:
                acc_ref[:, cols] += part

    @pl.when(j == 0)
    def _():
        xn_ref[...] = (_rms_scale(h_ref[...]) * pre_g_ref[...]).astype(BF16)
        swiglu_step(first=True)

    last = pl.num_programs(1) - 1

    @pl.when(jnp.logical_and(j > 0, j < last))
    def _():
        swiglu_step(first=False)

    @pl.when(j == last)
    def _():
        swiglu_step(first=False)
        out_ref[...] = h_ref[...] + 0.5 * (_rms_scale(acc_ref[...]) * post_g_ref[...])


def _ffn(h, pre_g, w_gate, w_up, w_down, post_g, *, tm, tf):
    t, d = h.shape
    f = w_gate.shape[1]
    assert t % tm == 0 and f % tf == 0 and f // tf >= 2, (t, tm, f, tf)
    return pl.pallas_call(
        _ffn_kernel,
        grid=(t // tm, f // tf),
        in_specs=[
            pl.BlockSpec((tm, d), lambda i, j: (i, 0)),
            pl.BlockSpec((1, d), lambda i, j: (0, 0)),
            pl.BlockSpec((d, tf), lambda i, j: (0, j)),
            pl.BlockSpec((d, tf), lambda i, j: (0, j)),
            pl.BlockSpec((tf, d), lambda i, j: (j, 0)),
            pl.BlockSpec((1, d), lambda i, j: (0, 0)),
        ],
        out_specs=pl.BlockSpec((tm, d), lambda i, j: (i, 0)),
        out_shape=jax.ShapeDtypeStruct((t, d), F32),
        scratch_shapes=[pltpu.VMEM((tm, d), BF16), pltpu.VMEM((tm, d), F32)],
        compiler_params=_params("parallel", "arbitrary"),
        name="ffn",
    )(h, pre_g, w_gate, w_up, w_down, post_g)


def _inproj_kernel(h_ref, g_ref, wa_ref, wg_ref, wdt_ref, cw_ref, cb_ref, out_ref, dt_ref,
                   xn_ref, tail_ref, *, j_k, j_z, j_x, j_g, tiles_per_seq):
    i = pl.program_id(0)
    j = pl.program_id(1)
    tm, tn = out_ref.shape
    tail = V7X_SUBLANES

    @pl.when(j == 0)
    def _():
        xn = (_rms_scale(h_ref[...]) * g_ref[...]).astype(BF16)
        xn_ref[...] = xn
        dt_ref[...] = jnp.dot(xn, wdt_ref[...], preferred_element_type=F32)

        @pl.when(i == 0)
        def _():
            tail_ref[...] = jnp.zeros(tail_ref.shape, F32)

    def project(w_ref, epilogue):
        xn = xn_ref[...]
        for c in range(tn // INPROJ_CHUNK):
            cols = slice(c * INPROJ_CHUNK, (c + 1) * INPROJ_CHUNK)
            epilogue(jnp.dot(xn, w_ref[:, cols], preferred_element_type=F32), cols)

    def elementwise(fn):
        def epilogue(y, cols):
            out_ref[:, cols] = fn(y).astype(out_ref.dtype)
        return epilogue

    def conv_silu(y, cols):
        jx = j - j_x
        rb = INPROJ_CONV_ROWS
        sub = lax.broadcasted_iota(jnp.int32, (rb, INPROJ_CHUNK), 0) % tail
        for r in range(tm // rb):
            r0 = r * rb
            cur = y[r0:r0 + rb]
            if r == 0:
                prev = jnp.concatenate([tail_ref[jx, :, cols], y[0:rb - tail]], axis=0)
            else:
                prev = y[r0 - tail:r0 - tail + rb]
            acc = cb_ref[:, cols] + cw_ref[SSM_CONV - 1:SSM_CONV, cols] * cur
            for s in range(1, SSM_CONV):
                k = SSM_CONV - 1 - s
                merged = jnp.where(sub >= tail - s, prev, cur).reshape(-1, tail, INPROJ_CHUNK)
                shifted = pltpu.roll(merged, s, 1).reshape(rb, INPROJ_CHUNK)
                acc = acc + cw_ref[k:k + 1, cols] * shifted
            out_ref[r0:r0 + rb, cols] = _silu(acc).astype(out_ref.dtype)
        seq_ends = (i % tiles_per_seq) == tiles_per_seq - 1
        tail_ref[jx, :, cols] = jnp.where(seq_ends, 0.0, y[tm - tail:tm])

    @pl.when(j < j_k)
    def _():
        project(wa_ref, elementwise(lambda y: y * (LOG2_E * ATT_HEAD_DIM ** -0.5)))

    @pl.when(jnp.logical_and(j >= j_k, j < j_z))
    def _():
        project(wa_ref, elementwise(lambda y: y))

    @pl.when(jnp.logical_and(j >= j_z, j < j_x))
    def _():
        project(wa_ref, elementwise(_silu))

    @pl.when(jnp.logical_and(j >= j_x, j < j_g))
    def _():
        project(wa_ref, conv_silu)

    @pl.when(j >= j_g)
    def _():
        project(wg_ref, elementwise(_sigmoid))


def _inproj(h, g, w_all, w_g, w_dt, conv_w, conv_b, *, tm, tn, seq, qk_w, kv_end, z_end, a_cols):
    t, d = h.shape
    na, ng = a_cols, w_g.shape[1]
    ndt = w_dt.shape[1]
    conv_dim = conv_w.shape[1]
    assert t % tm == 0 and seq % tm == 0, (t, seq, tm)
    assert all(v % tn == 0 for v in (qk_w, kv_end, z_end, na, ng, conv_dim)), (tn,)
    assert z_end + conv_dim == na and tn % INPROJ_CHUNK == 0 and na <= w_all.shape[1]
    j_k, j_z, j_x, j_g = qk_w // tn, kv_end // tn, z_end // tn, na // tn
    n_conv = conv_dim // tn
    return pl.pallas_call(
        functools.partial(_inproj_kernel, j_k=j_k, j_z=j_z, j_x=j_x, j_g=j_g,
                          tiles_per_seq=seq // tm),
        grid=(t // tm, (na + ng) // tn),
        in_specs=[
            pl.BlockSpec((tm, d), lambda i, j: (i, 0)),
            pl.BlockSpec((1, d), lambda i, j: (0, 0)),
            pl.BlockSpec((d, tn), lambda i, j: (0, jnp.minimum(j, j_g - 1))),
            pl.BlockSpec((d, tn), lambda i, j: (0, jnp.maximum(j - j_g, 0))),
            pl.BlockSpec((d, ndt), lambda i, j: (0, 0)),
            pl.BlockSpec((SSM_CONV, tn), lambda i, j: (0, jnp.clip(j - j_x, 0, n_conv - 1))),
            pl.BlockSpec((1, tn), lambda i, j: (0, jnp.clip(j - j_x, 0, n_conv - 1))),
        ],
        out_specs=[
            pl.BlockSpec((tm, tn), lambda i, j: (i, j)),
            pl.BlockSpec((tm, ndt), lambda i, j: (i, 0)),
        ],
        out_shape=[jax.ShapeDtypeStruct((t, na + ng), BF16), jax.ShapeDtypeStruct((t, ndt), F32)],
        scratch_shapes=[
            pltpu.VMEM((tm, d), BF16),
            pltpu.VMEM((n_conv, V7X_SUBLANES, tn), F32),
        ],
        compiler_params=_params("arbitrary", "arbitrary"),
        name="in_proj",
    )(h, g, w_all, w_g, w_dt, conv_w, conv_b)


def _attn_kernel(lq1_ref, lk1_ref, lq2_ref, lk2_ref, g_ref, q_ref, k_ref, v_ref, o_ref,
                 vt_ref, m_ref, l_ref, acc_ref, sa_ref, sb_ref, *, tq):
    qi = pl.program_id(2)
    dh = ATT_HEAD_DIM
    n_kv = vt_ref.shape[0]

    @pl.when(qi == 0)
    def _():
        for j in range(n_kv):
            vt_ref[j] = v_ref[0, j * tq:(j + 1) * tq, :].T

    m_ref[...] = jnp.full(m_ref.shape, -jnp.inf, F32)
    l_ref[...] = jnp.zeros(l_ref.shape, F32)
    acc_ref[...] = jnp.zeros(acc_ref.shape, F32)

    def scores(j, s_ref):
        start = pl.multiple_of(j * tq, tq)
        for mi in range(2):
            q = q_ref[0, :, mi * dh:(mi + 1) * dh]
            k = k_ref[0, pl.ds(start, tq), mi * dh:(mi + 1) * dh]
            s_ref[mi] = lax.dot_general(k, q, (((1,), (1,)), ((), ())), preferred_element_type=F32)

    def softmax_pv(j, s_ref, masked):
        vt = vt_ref[j]
        for mi in range(2):
            st = s_ref[mi]
            if masked:
                kpos = lax.broadcasted_iota(jnp.int32, st.shape, 0)
                qpos = lax.broadcasted_iota(jnp.int32, st.shape, 1)
                st = jnp.where(kpos <= qpos, st, -jnp.inf)
            m_prev = m_ref[mi]
            m_new = jnp.maximum(m_prev, jnp.max(st, axis=0, keepdims=True))
            alpha = jnp.exp2(m_prev - m_new)
            p = jnp.exp2(st - m_new)
            l_ref[mi] = alpha * l_ref[mi] + jnp.sum(p, axis=0, keepdims=True)
            acc_ref[mi] = alpha * acc_ref[mi] + jnp.dot(vt, p.astype(BF16), preferred_element_type=F32)
            m_ref[mi] = m_new

    scores(0, sa_ref)

    def tile_pair(i, carry):
        j = 2 * i
        scores(j + 1, sb_ref)
        softmax_pv(j, sa_ref, masked=False)
        scores(j + 2, sa_ref)
        softmax_pv(j + 1, sb_ref, masked=False)
        return carry

    lax.fori_loop(0, qi // 2, tile_pair, 0)

    def finalize():
        lam = (jnp.exp(jnp.sum(lq1_ref[...] * lk1_ref[...], axis=-1, keepdims=True))
               - jnp.exp(jnp.sum(lq2_ref[...] * lk2_ref[...], axis=-1, keepdims=True))
               + LAMBDA_INIT)
        ot = acc_ref[0] / l_ref[0] - lam * (acc_ref[1] / l_ref[1])
        ot = ot * lax.rsqrt(jnp.mean(ot * ot, axis=0, keepdims=True) + RMS_EPS)
        o_ref[0] = (ot.T * g_ref[...] * (1.0 - LAMBDA_INIT)).astype(o_ref.dtype)

    @pl.when(qi % 2 == 0)
    def _():
        softmax_pv(qi, sa_ref, masked=True)
        finalize()

    @pl.when(qi % 2 == 1)
    def _():
        scores(qi, sb_ref)
        softmax_pv(qi - 1, sa_ref, masked=False)
        softmax_pv(qi, sb_ref, masked=True)
        finalize()


def _attention(proj, lq1, lk1, lq2, lk2, subln_g, *, tq, q_col, k_col, v_col):
    b, s, _ = proj.shape
    w = ATT_V_DIM
    assert s % tq == 0, (s, tq)
    vec = pl.BlockSpec((1, ATT_HEAD_DIM), lambda bi, h, qi: (0, 0))
    return pl.pallas_call(
        functools.partial(_attn_kernel, tq=tq),
        grid=(b, ATT_HEADS, s // tq),
        in_specs=[
            vec, vec, vec, vec,
            pl.BlockSpec((1, w), lambda bi, h, qi: (0, 0)),
            pl.BlockSpec((1, tq, w), lambda bi, h, qi: (bi, qi, q_col + h)),
            pl.BlockSpec((1, s, w), lambda bi, h, qi: (bi, 0, k_col + h)),
            pl.BlockSpec((1, s, w), lambda bi, h, qi: (bi, 0, v_col + h)),
        ],
        out_specs=pl.BlockSpec((1, tq, w), lambda bi, h, qi: (bi, qi, h)),
        out_shape=jax.ShapeDtypeStruct((b, s, ATT_HEADS * w), BF16),
        scratch_shapes=[
            pltpu.VMEM((s // tq, w, tq), BF16),
            pltpu.VMEM((2, 1, tq), F32),
            pltpu.VMEM((2, 1, tq), F32),
            pltpu.VMEM((2, w, tq), F32),
            pltpu.VMEM((2, tq, tq), F32),
            pltpu.VMEM((2, tq, tq), F32),
        ],
        compiler_params=_params("parallel", "parallel", "arbitrary"),
        name="diff_attn",
    )(lq1, lk1, lq2, lk2, subln_g, proj, proj, proj)


def _split3(x):
    p1 = x.astype(BF16)
    r1 = x - p1.astype(F32)
    p2 = r1.astype(BF16)
    p3 = (r1 - p2.astype(F32)).astype(BF16)
    return p1, p2, p3


def _ssd_kernel(dt_ref, dtb_ref, alog_ref, *refs, n_parts):
    x_refs, (b_ref, c_ref) = refs[:n_parts], refs[n_parts:n_parts + 2]
    z_refs = refs[n_parts + 2:2 * n_parts + 2]
    dskip_ref, ng_ref, o_ref, state_ref = refs[2 * n_parts + 2:]
    c = pl.program_id(1)
    ln = SSM_CHUNK
    hp = SSM_HEAD_DIM
    ns = SSM_D_STATE
    n_grp = b_ref.shape[-1] // ns
    gw = o_ref.shape[-1] // n_grp
    heads = gw // hp
    part_grp = n_grp // n_parts

    row = lax.broadcasted_iota(jnp.int32, (ln, ln), 0)
    col = lax.broadcasted_iota(jnp.int32, (ln, ln), 1)
    tril = col <= row

    xb = dt_ref[0] + dtb_ref[...]
    dt_all = jnp.maximum(xb, 0.0) + jnp.log1p(jnp.exp(-jnp.abs(xb)))
    adt = dt_all * (-jnp.exp(alog_ref[...]))
    ltri = jnp.where(tril, 1.0, 0.0).astype(BF16)
    acs_all = sum(jnp.dot(ltri, part, preferred_element_type=F32) for part in _split3(adt))
    acst = acs_all.T

    lane = lax.broadcasted_iota(jnp.int32, (ln, 2 * hp), 1)
    lo = lane < hp
    hi = jnp.logical_not(lo)

    @pl.when(c == 0)
    def _():
        state_ref[...] = jnp.zeros(state_ref.shape, F32)

    for g in range(n_grp):
        gcols = slice(g * gw, (g + 1) * gw)
        pcols = slice((g % part_grp) * gw, (g % part_grp + 1) * gw)
        scols = slice(g * ns, (g + 1) * ns)
        xc = x_refs[g // part_grp][0, :, pcols].astype(F32)
        bc = b_ref[0, :, scols]
        cc = c_ref[0, :, scols]
        cb = lax.dot_general(cc, bc, (((1,), (1,)), ((), ())), preferred_element_type=F32)
        state = state_ref[g]
        y_off = jnp.dot(cc, state.astype(BF16), preferred_element_type=F32)

        y_parts = []
        xw_parts = []
        dec_last_parts = []
        for pr in range(heads // 2):
            sl = slice(pr * 2 * hp, (pr + 1) * 2 * hp)
            h0 = g * heads + 2 * pr
            acs_b = [jnp.broadcast_to(acs_all[:, h:h + 1], (ln, 2 * hp)) for h in (h0, h0 + 1)]
            dt_pair = jnp.where(lo, dt_all[:, h0:h0 + 1], dt_all[:, h0 + 1:h0 + 2])
            acs_pair = jnp.where(lo, acs_b[0], acs_b[1])
            xdt = xc[:, sl] * dt_pair
            y_pair = y_off[:, sl] * jnp.exp(acs_pair)
            for k, keep in enumerate((lo, hi)):
                rowv = acst[h0 + k:h0 + k + 1, :]
                decay = jnp.exp(jnp.where(tril, acs_b[k] - rowv, -jnp.inf))
                m = (cb * decay).astype(BF16)
                y_pair = y_pair + jnp.dot(m, jnp.where(keep, xdt, 0.0).astype(BF16),
                                          preferred_element_type=F32)
            y_parts.append(y_pair)
            last = acs_pair[ln - 1:ln, :]
            xw_parts.append((xdt * jnp.exp(last - acs_pair)).astype(BF16))
            dec_last_parts.append(jnp.exp(last))

        xw = jnp.concatenate(xw_parts, axis=1)
        dec_last = jnp.concatenate(dec_last_parts, axis=1)
        bct = bc.astype(F32).T.astype(BF16)
        state_ref[g] = state * dec_last + jnp.dot(bct, xw, preferred_element_type=F32)

        y = jnp.concatenate(y_parts, axis=1)
        y = (y + xc * dskip_ref[:, gcols]) * z_refs[g // part_grp][0, :, pcols].astype(F32)
        o_ref[0, :, gcols] = (_rms_scale(y) * ng_ref[:, gcols]).astype(o_ref.dtype)


def _ssd(proj, dt_raw, dt_bias, a_log, d_skip_row, norm_g, *, n_parts, x_col, b_col, c_col, z_col):
    bsz, s, _ = proj.shape
    ln = SSM_CHUNK
    d_inner = d_skip_row.shape[1]
    gw = d_inner // SSM_GROUPS
    xw, sw = d_inner // n_parts, SSM_GROUPS * SSM_D_STATE
    assert s % ln == 0 and gw % (2 * SSM_HEAD_DIM) == 0 and SSM_GROUPS % n_parts == 0, (s, ln, gw)
    assert x_col % xw == 0 and z_col % xw == 0 and b_col % sw == 0 and c_col % sw == 0

    def part(col, p):
        return pl.BlockSpec((1, ln, xw), lambda b, c: (b, c, col // xw + p))

    row128 = pl.BlockSpec((1, V7X_LANES), lambda b, c: (0, 0))
    return pl.pallas_call(
        functools.partial(_ssd_kernel, n_parts=n_parts),
        grid=(bsz, s // ln),
        in_specs=[
            pl.BlockSpec((1, ln, V7X_LANES), lambda b, c: (b, c, 0)),
            row128, row128,
            *[part(x_col, p) for p in range(n_parts)],
            pl.BlockSpec((1, ln, sw), lambda b, c: (b, c, b_col // sw)),
            pl.BlockSpec((1, ln, sw), lambda b, c: (b, c, c_col // sw)),
            *[part(z_col, p) for p in range(n_parts)],
            pl.BlockSpec((1, d_inner), lambda b, c: (0, 0)),
            pl.BlockSpec((1, d_inner), lambda b, c: (0, 0)),
        ],
        out_specs=pl.BlockSpec((1, ln, d_inner), lambda b, c: (b, c, 0)),
        out_shape=jax.ShapeDtypeStruct((bsz, s, d_inner), BF16),
        scratch_shapes=[pltpu.VMEM((SSM_GROUPS, SSM_D_STATE, gw), F32)],
        compiler_params=_params("parallel", "arbitrary"),
        name="ssd",
    )(dt_raw, dt_bias, a_log, *([proj] * n_parts), proj, proj, *([proj] * n_parts),
      d_skip_row, norm_g)


def _merge_kernel(a_ref, s_ref, wa_ref, ws_ref, ga_ref, gs_ref, out_ref):
    ya = jnp.dot(a_ref[...], wa_ref[...], preferred_element_type=F32)
    ys = jnp.dot(s_ref[...], ws_ref[...], preferred_element_type=F32)
    out_ref[...] = (ga_ref[...].astype(F32) * ya + gs_ref[...].astype(F32) * ys).astype(out_ref.dtype)


def _merge(o_att, o_ssm, w_att, w_ssm, proj, *, tm, tn, ga_col, gs_col):
    t, ka = o_att.shape
    ks = o_ssm.shape[1]
    n = w_att.shape[1]
    assert t % tm == 0 and n % tn == 0, (t, tm, n, tn)
    return pl.pallas_call(
        _merge_kernel,
        grid=(t // tm, n // tn),
        in_specs=[
            pl.BlockSpec((tm, ka), lambda i, j: (i, 0)),
            pl.BlockSpec((tm, ks), lambda i, j: (i, 0)),
            pl.BlockSpec((ka, tn), lambda i, j: (0, j)),
            pl.BlockSpec((ks, tn), lambda i, j: (0, j)),
            pl.BlockSpec((tm, tn), lambda i, j: (i, ga_col + j)),
            pl.BlockSpec((tm, tn), lambda i, j: (i, gs_col + j)),
        ],
        out_specs=pl.BlockSpec((tm, tn), lambda i, j: (i, j)),
        out_shape=jax.ShapeDtypeStruct((t, n), BF16),
        compiler_params=_params("parallel", "arbitrary"),
        name="merge",
    )(o_att, o_ssm, w_att, w_ssm, proj, proj)


def _outproj_kernel(m_ref, w_ref, h_ref, g_ref, out_ref):
    y = jnp.dot(m_ref[...], w_ref[...], preferred_element_type=F32)
    out_ref[...] = h_ref[...] + _rms_scale(y) * g_ref[...]


def _outproj(merged, w_out, h, post_g, *, tm):
    t, d = h.shape
    k = merged.shape[1]
    assert t % tm == 0, (t, tm)
    return pl.pallas_call(
        _outproj_kernel,
        grid=(t // tm,),
        in_specs=[
            pl.BlockSpec((tm, k), lambda i: (i, 0)),
            pl.BlockSpec((k, d), lambda i: (0, 0)),
            pl.BlockSpec((tm, d), lambda i: (i, 0)),
            pl.BlockSpec((1, d), lambda i: (0, 0)),
        ],
        out_specs=pl.BlockSpec((tm, d), lambda i: (i, 0)),
        out_shape=jax.ShapeDtypeStruct((t, d), F32),
        compiler_params=_params("parallel"),
        name="out_proj",
    )(merged, w_out, h, post_g)


def _pad_lanes(v):
    return jnp.pad(v.astype(F32), (0, V7X_LANES - v.shape[0]))[None, :]


def kernel(x, ffn1_pre_g, ffn1_w_gate, ffn1_w_up, ffn1_w_down, ffn1_post_g, mix_pre_g, w_in, att_lambda_q1, att_lambda_k1, att_lambda_q2, att_lambda_k2, att_subln_g, ssm_conv_w, ssm_conv_b, ssm_dt_bias, ssm_a_log, ssm_d, ssm_norm_g, w_branch_att, w_branch_ssm, w_out, mix_post_g, ffn2_pre_g, ffn2_w_gate, ffn2_w_up, ffn2_w_down, ffn2_post_g):
    bsz, seq, d = x.shape
    t = bsz * seq
    qk_w = ATT_HEADS * 2 * ATT_HEAD_DIM
    v_w = ATT_HEADS * ATT_V_DIM
    d_inner = ssm_norm_g.shape[1]
    n_heads = ssm_dt_bias.shape[1]
    bc_w = SSM_GROUPS * SSM_D_STATE
    conv_dim = d_inner + 2 * bc_w

    w = w_in[0].astype(BF16)
    o_q, o_k, o_v = 0, qk_w, 2 * qk_w
    o_z = o_v + v_w
    o_xbc = o_z + d_inner
    o_dt = o_xbc + conv_dim
    o_gate = o_dt + n_heads
    w_g = w[:, o_gate:]
    w_dt = jnp.pad(w[:, o_dt:o_gate], ((0, 0), (0, V7X_LANES - n_heads)))
    c_q, c_k, c_v, c_z, c_x = o_q, o_k, o_v, o_z, o_xbc
    c_b = c_x + d_inner
    c_c = c_b + bc_w
    c_ga = c_x + conv_dim
    c_gs = c_ga + d

    h = x.reshape(t, d)
    h = _ffn(h, ffn1_pre_g, ffn1_w_gate[0].astype(BF16), ffn1_w_up[0].astype(BF16),
             ffn1_w_down[0].astype(BF16), ffn1_post_g, tm=FFN_TM, tf=FFN_TF)

    proj, dt_raw = _inproj(h, mix_pre_g, w, w_g, w_dt, ssm_conv_w[0], ssm_conv_b,
                           tm=INPROJ_TM, tn=INPROJ_TN, seq=seq, qk_w=qk_w, kv_end=o_z,
                           z_end=o_xbc, a_cols=o_dt)
    proj3 = proj.reshape(bsz, seq, proj.shape[1])

    o_att = _attention(proj3, att_lambda_q1, att_lambda_k1, att_lambda_q2, att_lambda_k2,
                       att_subln_g, tq=ATTN_TQ,
                       q_col=c_q // ATT_V_DIM, k_col=c_k // ATT_V_DIM, v_col=c_v // ATT_V_DIM)

    d_skip_row = jnp.repeat(ssm_d[0].astype(F32), SSM_HEAD_DIM)[None, :]
    o_ssm = _ssd(proj3, dt_raw.reshape(bsz, seq, V7X_LANES), _pad_lanes(ssm_dt_bias[0]),
                 _pad_lanes(ssm_a_log[0]), d_skip_row, ssm_norm_g,
                 n_parts=SSD_COL_PARTS, x_col=c_x, b_col=c_b, c_col=c_c, z_col=c_z)

    merged = _merge(o_att.reshape(t, v_w), o_ssm.reshape(t, d_inner),
                    w_branch_att[0].astype(BF16), w_branch_ssm[0].astype(BF16), proj,
                    tm=MERGE_TM, tn=MERGE_TN, ga_col=c_ga // MERGE_TN, gs_col=c_gs // MERGE_TN)
    h = _outproj(merged, w_out[0].astype(BF16), h, mix_post_g, tm=OUTPROJ_TM)

    h = _ffn(h, ffn2_pre_g, ffn2_w_gate[0].astype(BF16), ffn2_w_up[0].astype(BF16),
             ffn2_w_down[0].astype(BF16), ffn2_post_g, tm=FFN_TM, tf=FFN_TF)
    return h.reshape(bsz, seq, d)
```

```python
import functools
import math

import jax
import jax.numpy as jnp
from jax import lax
from jax.experimental import pallas as pl
from jax.experimental.pallas import tpu as pltpu

F32 = jnp.float32
BF16 = jnp.bfloat16

V7X_SUBLANES = 8
V7X_LANES = 128
V7X_VMEM_LIMIT_BYTES = 56 * 1024 * 1024

RMS_EPS = 1e-6

ATT_HEADS = 8
ATT_HEAD_DIM = 128
ATT_V_DIM = 2 * ATT_HEAD_DIM
SSM_HEAD_DIM = 64
SSM_GROUPS = 8
SSM_D_STATE = 128
SSM_CONV = 4
SSM_CHUNK = 128
LAMBDA_INIT = 0.8 - 0.6 * math.exp(-0.3 * 0)
LOG2_E = math.log2(math.e)

FFN_TM, FFN_TF = 1024, 512
INPROJ_TM, INPROJ_TN = 1024, 1024
ATTN_TQ = 512
SSD_COL_PARTS = 2
MERGE_TM, MERGE_TN = 1024, 512
OUTPROJ_TM = 512

FFN_DOWN_CHUNK = 512
INPROJ_CHUNK = 256
INPROJ_CONV_ROWS = 128


def _params(*semantics):
    return pltpu.CompilerParams(dimension_semantics=semantics,
                                vmem_limit_bytes=V7X_VMEM_LIMIT_BYTES)


def _rms_scale(x):
    return x * lax.rsqrt(jnp.mean(x * x, axis=-1, keepdims=True) + RMS_EPS)


def _silu(x):
    return x / (1.0 + jnp.exp(-x))


def _sigmoid(x):
    return 1.0 / (1.0 + jnp.exp(-x))


def _ffn_kernel(h_ref, pre_g_ref, wg_ref, wu_ref, wd_ref, post_g_ref, out_ref, xn_ref, acc_ref):
    j = pl.program_id(1)

    def swiglu_step(first):
        xn = xn_ref[...]
        gate = jnp.dot(xn, wg_ref[...], preferred_element_type=F32)
        up = jnp.dot(xn, wu_ref[...], preferred_element_type=F32)
        act = (_silu(gate) * up).astype(BF16)
        d = acc_ref.shape[1]
        for c in range(d // FFN_DOWN_CHUNK):
            cols = slice(c * FFN_DOWN_CHUNK, (c + 1) * FFN_DOWN_CHUNK)
            part = jnp.dot(act, wd_ref[:, cols], preferred_element_type=F32)
            if first:
                acc_ref[:, cols] = part
            else:
                acc_ref[:, cols] += part

    @pl.when(j == 0)
    def _():
        xn_ref[...] = (_rms_scale(h_ref[...]) * pre_g_ref[...]).astype(BF16)
        swiglu_step(first=True)

    last = pl.num_programs(1) - 1

    @pl.when(jnp.logical_and(j > 0, j < last))
    def _():
        swiglu_step(first=False)

    @pl.when(j == last)
    def _():
        swiglu_step(first=False)
        out_ref[...] = h_ref[...] + 0.5 * (_rms_scale(acc_ref[...]) * post_g_ref[...])


def _ffn(h, pre_g, w_gate, w_up, w_down, post_g, *, tm, tf):
    t, d = h.shape
    f = w_gate.shape[1]
    assert t % tm == 0 and f % tf == 0 and f // tf >= 2, (t, tm, f, tf)
    return pl.pallas_call(
        _ffn_kernel,
        grid=(t // tm, f // tf),
        in_specs=[
            pl.BlockSpec((tm, d), lambda i, j: (i, 0)),
            pl.BlockSpec((1, d), lambda i, j: (0, 0)),
            pl.BlockSpec((d, tf), lambda i, j: (0, j)),
            pl.BlockSpec((d, tf), lambda i, j: (0, j)),
            pl.BlockSpec((tf, d), lambda i, j: (j, 0)),
            pl.BlockSpec((1, d), lambda i, j: (0, 0)),
        ],
        out_specs=pl.BlockSpec((tm, d), lambda i, j: (i, 0), pipeline_mode=pl.Buffered(1)),
        out_shape=jax.ShapeDtypeStruct((t, d), F32),
        scratch_shapes=[pltpu.VMEM((tm, d), BF16), pltpu.VMEM((tm, d), F32)],
        compiler_params=_params("parallel", "arbitrary"),
        name="ffn",
    )(h, pre_g, w_gate, w_up, w_down, post_g)


def _inproj_kernel(h_ref, g_ref, wa_ref, wg_ref, wdt_ref, cw_ref, cb_ref, out_ref, dt_ref,
                   xn_ref, tail_ref, *, j_k, j_z, j_x, j_g, tiles_per_seq):
    i = pl.program_id(0)
    j = pl.program_id(1)
    tm, tn = out_ref.shape
    tail = V7X_SUBLANES

    @pl.when(j == 0)
    def _():
        xn = (_rms_scale(h_ref[...]) * g_ref[...]).astype(BF16)
        xn_ref[...] = xn
        dt_ref[...] = jnp.dot(xn, wdt_ref[...], preferred_element_type=F32)

        @pl.when(i == 0)
        def _():
            tail_ref[...] = jnp.zeros(tail_ref.shape, F32)

    def project(w_ref, epilogue):
        xn = xn_ref[...]
        for c in range(tn // INPROJ_CHUNK):
            cols = slice(c * INPROJ_CHUNK, (c + 1) * INPROJ_CHUNK)
            epilogue(jnp.dot(xn, w_ref[:, cols], preferred_element_type=F32), cols)

    def elementwise(fn):
        def epilogue(y, cols):
            out_ref[:, cols] = fn(y).astype(out_ref.dtype)
        return epilogue

    def conv_silu(y, cols):
        jx = j - j_x
        rb = INPROJ_CONV_ROWS
        sub = lax.broadcasted_iota(jnp.int32, (rb, INPROJ_CHUNK), 0) % tail
        for r in range(tm // rb):
            r0 = r * rb
            cur = y[r0:r0 + rb]
            if r == 0:
                prev = jnp.concatenate([tail_ref[jx, :, cols], y[0:rb - tail]], axis=0)
            else:
                prev = y[r0 - tail:r0 - tail + rb]
            acc = cb_ref[:, cols] + cw_ref[SSM_CONV - 1:SSM_CONV, cols] * cur
            for s in range(1, SSM_CONV):
                k = SSM_CONV - 1 - s
                merged = jnp.where(sub >= tail - s, prev, cur).reshape(-1, tail, INPROJ_CHUNK)
                shifted = pltpu.roll(merged, s, 1).reshape(rb, INPROJ_CHUNK)
                acc = acc + cw_ref[k:k + 1, cols] * shifted
            out_ref[r0:r0 + rb, cols] = _silu(acc).astype(out_ref.dtype)
        seq_ends = (i % tiles_per_seq) == tiles_per_seq - 1
        tail_ref[jx, :, cols] = jnp.where(seq_ends, 0.0, y[tm - tail:tm])

    @pl.when(j < j_k)
    def _():
        project(wa_ref, elementwise(lambda y: y * (LOG2_E * ATT_HEAD_DIM ** -0.5)))

    @pl.when(jnp.logical_and(j >= j_k, j < j_z))
    def _():
        project(wa_ref, elementwise(lambda y: y))

    @pl.when(jnp.logical_and(j >= j_z, j < j_x))
    def _():
        project(wa_ref, elementwise(_silu))

    @pl.when(jnp.logical_and(j >= j_x, j < j_g))
    def _():
        project(wa_ref, conv_silu)

    @pl.when(j >= j_g)
    def _():
        project(wg_ref, elementwise(_sigmoid))


def _inproj(h, g, w_all, w_g, w_dt, conv_w, conv_b, *, tm, tn, seq, qk_w, kv_end, z_end, a_cols):
    t, d = h.shape
    na, ng = a_cols, w_g.shape[1]
    ndt = w_dt.shape[1]
    conv_dim = conv_w.shape[1]
    assert t % tm == 0 and seq % tm == 0, (t, seq, tm)
    assert all(v % tn == 0 for v in (qk_w, kv_end, z_end, na, ng, conv_dim)), (tn,)
    assert z_end + conv_dim == na and tn % INPROJ_CHUNK == 0 and na <= w_all.shape[1]
    j_k, j_z, j_x, j_g = qk_w // tn, kv_end // tn, z_end // tn, na // tn
    n_conv = conv_dim // tn
    return pl.pallas_call(
        functools.partial(_inproj_kernel, j_k=j_k, j_z=j_z, j_x=j_x, j_g=j_g,
                          tiles_per_seq=seq // tm),
        grid=(t // tm, (na + ng) // tn),
        in_specs=[
            pl.BlockSpec((tm, d), lambda i, j: (i, 0)),
            pl.BlockSpec((1, d), lambda i, j: (0, 0)),
            pl.BlockSpec((d, tn), lambda i, j: (0, jnp.minimum(j, j_g - 1))),
            pl.BlockSpec((d, tn), lambda i, j: (0, jnp.maximum(j - j_g, 0))),
            pl.BlockSpec((d, ndt), lambda i, j: (0, 0)),
            pl.BlockSpec((SSM_CONV, tn), lambda i, j: (0, jnp.clip(j - j_x, 0, n_conv - 1))),
            pl.BlockSpec((1, tn), lambda i, j: (0, jnp.clip(j - j_x, 0, n_conv - 1))),
        ],
        out_specs=[
            pl.BlockSpec((tm, tn), lambda i, j: (i, j)),
            pl.BlockSpec((tm, ndt), lambda i, j: (i, 0)),
        ],
        out_shape=[jax.ShapeDtypeStruct((t, na + ng), BF16), jax.ShapeDtypeStruct((t, ndt), F32)],
        scratch_shapes=[
            pltpu.VMEM((tm, d), BF16),
            pltpu.VMEM((n_conv, V7X_SUBLANES, tn), F32),
        ],
        compiler_params=_params("arbitrary", "arbitrary"),
        name="in_proj",
    )(h, g, w_all, w_g, w_dt, conv_w, conv_b)


def _attn_kernel(lq1_ref, lk1_ref, lq2_ref, lk2_ref, g_ref, q_ref, k_ref, v_ref, o_ref,
                 vt_ref, m_ref, l_ref, acc_ref, sa_ref, sb_ref, *, tq):
    qi = pl.program_id(2)
    dh = ATT_HEAD_DIM
    n_kv = vt_ref.shape[0]

    @pl.when(qi == 0)
    def _():
        for j in range(n_kv):
            vt_ref[j] = v_ref[0, j * tq:(j + 1) * tq, :].T

    m_ref[...] = jnp.full(m_ref.shape, -jnp.inf, F32)
    l_ref[...] = jnp.zeros(l_ref.shape, F32)
    acc_ref[...] = jnp.zeros(acc_ref.shape, F32)

    def scores(j, s_ref):
        start = pl.multiple_of(j * tq, tq)
        for mi in range(2):
            q = q_ref[0, :, mi * dh:(mi + 1) * dh]
            k = k_ref[0, pl.ds(start, tq), mi * dh:(mi + 1) * dh]
            s_ref[mi] = lax.dot_general(k, q, (((1,), (1,)), ((), ())), preferred_element_type=F32)

    def softmax_pv(j, s_ref, masked):
        vt = vt_ref[j]
        for mi in range(2):
            st = s_ref[mi]
            if masked:
                kpos = lax.broadcasted_iota(jnp.int32, st.shape, 0)
                qpos = lax.broadcasted_iota(jnp.int32, st.shape, 1)
                st = jnp.where(kpos <= qpos, st, -jnp.inf)
            m_prev = m_ref[mi]
            m_new = jnp.maximum(m_prev, jnp.max(st, axis=0, keepdims=True))
            alpha = jnp.exp2(m_prev - m_new)
            p = jnp.exp2(st - m_new)
            l_ref[mi] = alpha * l_ref[mi] + jnp.sum(p, axis=0, keepdims=True)
            acc_ref[mi] = alpha * acc_ref[mi] + jnp.dot(vt, p.astype(BF16), preferred_element_type=F32)
            m_ref[mi] = m_new

    scores(0, sa_ref)

    def tile_pair(i, carry):
        j = 2 * i
        scores(j + 1, sb_ref)
        softmax_pv(j, sa_ref, masked=False)
        scores(j + 2, sa_ref)
        softmax_pv(j + 1, sb_ref, masked=False)
        return carry

    lax.fori_loop(0, qi // 2, tile_pair, 0)

    def finalize():
        lam = (jnp.exp(jnp.sum(lq1_ref[...] * lk1_ref[...], axis=-1, keepdims=True))
               - jnp.exp(jnp.sum(lq2_ref[...] * lk2_ref[...], axis=-1, keepdims=True))
               + LAMBDA_INIT)
        ot = acc_ref[0] / l_ref[0] - lam * (acc_ref[1] / l_ref[1])
        ot = ot * lax.rsqrt(jnp.mean(ot * ot, axis=0, keepdims=True) + RMS_EPS)
        o_ref[0] = (ot.T * g_ref[...] * (1.0 - LAMBDA_INIT)).astype(o_ref.dtype)

    @pl.when(qi % 2 == 0)
    def _():
        softmax_pv(qi, sa_ref, masked=True)
        finalize()

    @pl.when(qi % 2 == 1)
    def _():
        scores(qi, sb_ref)
        softmax_pv(qi - 1, sa_ref, masked=False)
        softmax_pv(qi, sb_ref, masked=True)
        finalize()


def _attention(proj, lq1, lk1, lq2, lk2, subln_g, *, tq, q_col, k_col, v_col):
    b, s, _ = proj.shape
    w = ATT_V_DIM
    assert s % tq == 0, (s, tq)
    vec = pl.BlockSpec((1, ATT_HEAD_DIM), lambda bi, h, qi: (0, 0))
    return pl.pallas_call(
        functools.partial(_attn_kernel, tq=tq),
        grid=(b, ATT_HEADS, s // tq),
        in_specs=[
            vec, vec, vec, vec,
            pl.BlockSpec((1, w), lambda bi, h, qi: (0, 0)),
            pl.BlockSpec((1, tq, w), lambda bi, h, qi: (bi, qi, q_col + h)),
            pl.BlockSpec((1, s, w), lambda bi, h, qi: (bi, 0, k_col + h)),
            pl.BlockSpec((1, s, w), lambda bi, h, qi: (bi, 0, v_col + h)),
        ],
        out_specs=pl.BlockSpec((1, tq, w), lambda bi, h, qi: (bi, qi, h)),
        out_shape=jax.ShapeDtypeStruct((b, s, ATT_HEADS * w), BF16),
        scratch_shapes=[
            pltpu.VMEM((s // tq, w, tq), BF16),
            pltpu.VMEM((2, 1, tq), F32),
            pltpu.VMEM((2, 1, tq), F32),
            pltpu.VMEM((2, w, tq), F32),
            pltpu.VMEM((2, tq, tq), F32),
            pltpu.VMEM((2, tq, tq), F32),
        ],
        compiler_params=_params("parallel", "parallel", "arbitrary"),
        name="diff_attn",
    )(lq1, lk1, lq2, lk2, subln_g, proj, proj, proj)


def _split3(x):
    p1 = x.astype(BF16)
    r1 = x - p1.astype(F32)
    p2 = r1.astype(BF16)
    p3 = (r1 - p2.astype(F32)).astype(BF16)
    return p1, p2, p3


def _ssd_kernel(dt_ref, dtb_ref, alog_ref, *refs, n_parts):
    x_refs, (b_ref, c_ref) = refs[:n_parts], refs[n_parts:n_parts + 2]
    z_refs = refs[n_parts + 2:2 * n_parts + 2]
    dskip_ref, ng_ref, o_ref, state_ref = refs[2 * n_parts + 2:]
    c = pl.program_id(1)
    ln = SSM_CHUNK
    hp = SSM_HEAD_DIM
    ns = SSM_D_STATE
    n_grp = b_ref.shape[-1] // ns
    gw = o_ref.shape[-1] // n_grp
    heads = gw // hp
    part_grp = n_grp // n_parts

    row = lax.broadcasted_iota(jnp.int32, (ln, ln), 0)
    col = lax.broadcasted_iota(jnp.int32, (ln, ln), 1)
    tril = col <= row

    xb = dt_ref[0] + dtb_ref[...]
    dt_all = jnp.maximum(xb, 0.0) + jnp.log1p(jnp.exp(-jnp.abs(xb)))
    adt = dt_all * (-jnp.exp(alog_ref[...]))
    ltri = jnp.where(tril, 1.0, 0.0).astype(BF16)
    acs_all = sum(jnp.dot(ltri, part, preferred_element_type=F32) for part in _split3(adt))
    acst = acs_all.T

    lane = lax.broadcasted_iota(jnp.int32, (ln, 2 * hp), 1)
    lo = lane < hp
    hi = jnp.logical_not(lo)

    @pl.when(c == 0)
    def _():
        state_ref[...] = jnp.zeros(state_ref.shape, F32)

    for g in range(n_grp):
        gcols = slice(g * gw, (g + 1) * gw)
        pcols = slice((g % part_grp) * gw, (g % part_grp + 1) * gw)
        scols = slice(g * ns, (g + 1) * ns)
        xc = x_refs[g // part_grp][0, :, pcols].astype(F32)
        bc = b_ref[0, :, scols]
        cc = c_ref[0, :, scols]
        cb = lax.dot_general(cc, bc, (((1,), (1,)), ((), ())), preferred_element_type=F32)
        state = state_ref[g]
        y_off = jnp.dot(cc, state.astype(BF16), preferred_element_type=F32)

        y_parts = []
        xw_parts = []
        dec_last_parts = []
        for pr in range(heads // 2):
            sl = slice(pr * 2 * hp, (pr + 1) * 2 * hp)
            h0 = g * heads + 2 * pr
            acs_b = [jnp.broadcast_to(acs_all[:, h:h + 1], (ln, 2 * hp)) for h in (h0, h0 + 1)]
            dt_pair = jnp.where(lo, dt_all[:, h0:h0 + 1], dt_all[:, h0 + 1:h0 + 2])
            acs_pair = jnp.where(lo, acs_b[0], acs_b[1])
            xdt = xc[:, sl] * dt_pair
            y_pair = y_off[:, sl] * jnp.exp(acs_pair)
            for k, keep in enumerate((lo, hi)):
                rowv = acst[h0 + k:h0 + k + 1, :]
                decay = jnp.exp(jnp.where(tril, acs_b[k] - rowv, -jnp.inf))
                m = (cb * decay).astype(BF16)
                y_pair = y_pair + jnp.dot(m, jnp.where(keep, xdt, 0.0).astype(BF16),
                                          preferred_element_type=F32)
            y_parts.append(y_pair)
            last = acs_pair[ln - 1:ln, :]
            xw_parts.append((xdt * jnp.exp(last - acs_pair)).astype(BF16))
            dec_last_parts.append(jnp.exp(last))

        xw = jnp.concatenate(xw_parts, axis=1)
        dec_last = jnp.concatenate(dec_last_parts, axis=1)
        bct = bc.astype(F32).T.astype(BF16)
        state_ref[g] = state * dec_last + jnp.dot(bct, xw, preferred_element_type=F32)

        y = jnp.concatenate(y_parts, axis=1)
        y = (y + xc * dskip_ref[:, gcols]) * z_refs[g // part_grp][0, :, pcols].astype(F32)
        o_ref[0, :, gcols] = (_rms_scale(y) * ng_ref[:, gcols]).astype(o_ref.dtype)


def _ssd(proj, dt_raw, dt_bias, a_log, d_skip_row, norm_g, *, n_parts, x_col, b_col, c_col, z_col):
    bsz, s, _ = proj.shape
    ln = SSM_CHUNK
    d_inner = d_skip_row.shape[1]
    gw = d_inner // SSM_GROUPS
    xw, sw = d_inner // n_parts, SSM_GROUPS * SSM_D_STATE
    assert s % ln == 0 and gw % (2 * SSM_HEAD_DIM) == 0 and SSM_GROUPS % n_parts == 0, (s, ln, gw)
    assert x_col % xw == 0 and z_col % xw == 0 and b_col % sw == 0 and c_col % sw == 0

    def part(col, p):
        return pl.BlockSpec((1, ln, xw), lambda b, c: (b, c, col // xw + p))

    row128 = pl.BlockSpec((1, V7X_LANES), lambda b, c: (0, 0))
    return pl.pallas_call(
        functools.partial(_ssd_kernel, n_parts=n_parts),
        grid=(bsz, s // ln),
        in_specs=[
            pl.BlockSpec((1, ln, V7X_LANES), lambda b, c: (b, c, 0)),
            row128, row128,
            *[part(x_col, p) for p in range(n_parts)],
            pl.BlockSpec((1, ln, sw), lambda b, c: (b, c, b_col // sw)),
            pl.BlockSpec((1, ln, sw), lambda b, c: (b, c, c_col // sw)),
            *[part(z_col, p) for p in range(n_parts)],
            pl.BlockSpec((1, d_inner), lambda b, c: (0, 0)),
            pl.BlockSpec((1, d_inner), lambda b, c: (0, 0)),
        ],
        out_specs=pl.BlockSpec((1, ln, d_inner), lambda b, c: (b, c, 0)),
        out_shape=jax.ShapeDtypeStruct((bsz, s, d_inner), BF16),
        scratch_shapes=[pltpu.VMEM((SSM_GROUPS, SSM_D_STATE, gw), F32)],
        compiler_params=_params("parallel", "arbitrary"),
        name="ssd",
    )(dt_raw, dt_bias, a_log, *([proj] * n_parts), proj, proj, *([proj] * n_parts),
      d_skip_row, norm_g)


def _merge_kernel(a_ref, s_ref, wa_ref, ws_ref, ga_ref, gs_ref, out_ref):
    ya = jnp.dot(a_ref[...], wa_ref[...], preferred_element_type=F32)
    ys = jnp.dot(s_ref[...], ws_ref[...], preferred_element_type=F32)
    out_ref[...] = (ga_ref[...].astype(F32) * ya + gs_ref[...].astype(F32) * ys).astype(out_ref.dtype)


def _merge(o_att, o_ssm, w_att, w_ssm, proj, *, tm, tn, ga_col, gs_col):
    t, ka = o_att.shape
    ks = o_ssm.shape[1]
    n = w_att.shape[1]
    assert t % tm == 0 and n % tn == 0, (t, tm, n, tn)
    return pl.pallas_call(
        _merge_kernel,
        grid=(t // tm, n // tn),
        in_specs=[
            pl.BlockSpec((tm, ka), lambda i, j: (i, 0)),
            pl.BlockSpec((tm, ks), lambda i, j: (i, 0)),
            pl.BlockSpec((ka, tn), lambda i, j: (0, j)),
            pl.BlockSpec((ks, tn), lambda i, j: (0, j)),
            pl.BlockSpec((tm, tn), lambda i, j: (i, ga_col + j)),
            pl.BlockSpec((tm, tn), lambda i, j: (i, gs_col + j)),
        ],
        out_specs=pl.BlockSpec((tm, tn), lambda i, j: (i, j)),
        out_shape=jax.ShapeDtypeStruct((t, n), BF16),
        compiler_params=_params("parallel", "arbitrary"),
        name="merge",
    )(o_att, o_ssm, w_att, w_ssm, proj, proj)


def _outproj_kernel(m_ref, w_ref, h_ref, g_ref, out_ref):
    y = jnp.dot(m_ref[...], w_ref[...], preferred_element_type=F32)
    out_ref[...] = h_ref[...] + _rms_scale(y) * g_ref[...]


def _outproj(merged, w_out, h, post_g, *, tm):
    t, d = h.shape
    k = merged.shape[1]
    assert t % tm == 0, (t, tm)
    return pl.pallas_call(
        _outproj_kernel,
        grid=(t // tm,),
        in_specs=[
            pl.BlockSpec((tm, k), lambda i: (i, 0)),
            pl.BlockSpec((k, d), lambda i: (0, 0)),
            pl.BlockSpec((tm, d), lambda i: (i, 0)),
            pl.BlockSpec((1, d), lambda i: (0, 0)),
        ],
        out_specs=pl.BlockSpec((tm, d), lambda i: (i, 0)),
        out_shape=jax.ShapeDtypeStruct((t, d), F32),
        compiler_params=_params("parallel"),
        name="out_proj",
    )(merged, w_out, h, post_g)


def _pad_lanes(v):
    return jnp.pad(v.astype(F32), (0, V7X_LANES - v.shape[0]))[None, :]


def kernel(x, ffn1_pre_g, ffn1_w_gate, ffn1_w_up, ffn1_w_down, ffn1_post_g, mix_pre_g, w_in, att_lambda_q1, att_lambda_k1, att_lambda_q2, att_lambda_k2, att_subln_g, ssm_conv_w, ssm_conv_b, ssm_dt_bias, ssm_a_log, ssm_d, ssm_norm_g, w_branch_att, w_branch_ssm, w_out, mix_post_g, ffn2_pre_g, ffn2_w_gate, ffn2_w_up, ffn2_w_down, ffn2_post_g):
    bsz, seq, d = x.shape
    t = bsz * seq
    qk_w = ATT_HEADS * 2 * ATT_HEAD_DIM
    v_w = ATT_HEADS * ATT_V_DIM
    d_inner = ssm_norm_g.shape[1]
    n_heads = ssm_dt_bias.shape[1]
    bc_w = SSM_GROUPS * SSM_D_STATE
    conv_dim = d_inner + 2 * bc_w

    w = w_in[0].astype(BF16)
    o_q, o_k, o_v = 0, qk_w, 2 * qk_w
    o_z = o_v + v_w
    o_xbc = o_z + d_inner
    o_dt = o_xbc + conv_dim
    o_gate = o_dt + n_heads
    w_g = w[:, o_gate:]
    w_dt = jnp.pad(w[:, o_dt:o_gate], ((0, 0), (0, V7X_LANES - n_heads)))
    c_q, c_k, c_v, c_z, c_x = o_q, o_k, o_v, o_z, o_xbc
    c_b = c_x + d_inner
    c_c = c_b + bc_w
    c_ga = c_x + conv_dim
    c_gs = c_ga + d

    h = x.reshape(t, d)
    h = _ffn(h, ffn1_pre_g, ffn1_w_gate[0].astype(BF16), ffn1_w_up[0].astype(BF16),
             ffn1_w_down[0].astype(BF16), ffn1_post_g, tm=FFN_TM, tf=FFN_TF)

    proj, dt_raw = _inproj(h, mix_pre_g, w, w_g, w_dt, ssm_conv_w[0], ssm_conv_b,
                           tm=INPROJ_TM, tn=INPROJ_TN, seq=seq, qk_w=qk_w, kv_end=o_z,
                           z_end=o_xbc, a_cols=o_dt)
    proj3 = proj.reshape(bsz, seq, proj.shape[1])

    o_att = _attention(proj3, att_lambda_q1, att_lambda_k1, att_lambda_q2, att_lambda_k2,
                       att_subln_g, tq=ATTN_TQ,
                       q_col=c_q // ATT_V_DIM, k_col=c_k // ATT_V_DIM, v_col=c_v // ATT_V_DIM)

    d_skip_row = jnp.repeat(ssm_d[0].astype(F32), SSM_HEAD_DIM)[None, :]
    o_ssm = _ssd(proj3, dt_raw.reshape(bsz, seq, V7X_LANES), _pad_lanes(ssm_dt_bias[0]),
                 _pad_lanes(ssm_a_log[0]), d_skip_row, ssm_norm_g,
                 n_parts=SSD_COL_PARTS, x_col=c_x, b_col=c_b, c_col=c_c, z_col=c_z)

    merged = _merge(o_att.reshape(t, v_w), o_ssm.reshape(t, d_inner),
                    w_branch_att[0].astype(BF16), w_branch_ssm[0].astype(BF16), proj,
                    tm=MERGE_TM, tn=MERGE_TN, ga_col=c_ga // MERGE_TN, gs_col=c_gs // MERGE_TN)
    h = _outproj(merged, w_out[0].astype(BF16), h, mix_post_g, tm=OUTPROJ_TM)

    h = _ffn(h, ffn2_pre_g, ffn2_w_gate[0].astype(BF16), ffn2_w_up[0].astype(BF16),
             ffn2_w_down[0].astype(BF16), ffn2_post_g, tm=FFN_TM, tf=FFN_TF)
    return h.reshape(bsz, seq, d)
```
